```python
import math
import jax, jax.numpy as jnp
from jax import lax
import numpy as np

D_MODEL = 1024
BATCH = 32
SEQ = 2048
DEPTH = 1

CTX_LEN = 256
GRID_W = 64
ATTN_HEADS = 4
ATTN_HEAD_DIM = 64
ATTN_V_DIM = 2 * ATTN_HEAD_DIM
ATTN_QK_W = ATTN_HEADS * 2 * ATTN_HEAD_DIM
ATTN_V_W = ATTN_HEADS * ATTN_V_DIM
CONV_W = 512
CONV_K = 3
MIX_W = ATTN_V_W + CONV_W
PROJ_W = 2 * ATTN_QK_W + ATTN_V_W + 3 * CONV_W
Q_BLOCK = 128
ROPE_BASE = 10000.0
ROT_AXIS = ATTN_HEAD_DIM // 2
N_EXPERTS = 16
CAP_FACTOR = 2
D_EXPERT = 1024
EPS = 1e-6

kernel_name = "hybrid_diffattn_shortconv_ecmoe_dit"


def rmsnorm(x, g):
    xf = x.astype(jnp.float32)
    y = xf * lax.rsqrt(jnp.mean(xf * xf, axis=-1, keepdims=True) + EPS)
    return (y * g.astype(jnp.float32)).astype(x.dtype)


def modulate(x, shift, scale):
    return x * (1 + scale) + shift


def axial_rope_tables(n_tok):
    rows = n_tok // GRID_W
    row = jnp.repeat(jnp.arange(rows, dtype=jnp.float32), GRID_W)
    col = jnp.tile(jnp.arange(GRID_W, dtype=jnp.float32), rows)
    inv = ROPE_BASE ** (-jnp.arange(0, ROT_AXIS, 2, dtype=jnp.float32) / ROT_AXIS)
    ang_r = row[:, None] * inv[None, :]
    ang_c = col[:, None] * inv[None, :]
    ang = jnp.concatenate([ang_r, ang_r, ang_c, ang_c], axis=-1)
    return jnp.cos(ang), jnp.sin(ang)


def apply_rope(x, cos, sin):
    q = ROT_AXIS // 2
    a, b, c2, d2 = x[..., :q], x[..., q:2 * q], x[..., 2 * q:3 * q], x[..., 3 * q:]
    rot = jnp.concatenate([-b, a, -d2, c2], axis=-1)
    cs = cos[None, :, None, None, :].astype(x.dtype)
    sn = sin[None, :, None, None, :].astype(x.dtype)
    return x * cs + rot * sn


def diff_attend(q, k, v, lam):
    s = jnp.einsum('bqhmd,bkhmd->bhmqk', q, k, preferred_element_type=jnp.float32)
    a = jax.nn.softmax(s * (1.0 / math.sqrt(ATTN_HEAD_DIM)), axis=-1)
    a = a[:, :, 0] - lam * a[:, :, 1]
    return jnp.einsum('bhqk,bkhe->bqhe', a.astype(v.dtype), v)


def latent_diff_attention(q, k_all, v_all, lam):
    b, t, h, _, d = q.shape
    nblk = t // Q_BLOCK
    qb = q.reshape(b, nblk, Q_BLOCK, h, 2, d).transpose(1, 0, 2, 3, 4, 5)
    o = lax.map(lambda qblk: diff_attend(qblk, k_all, v_all, lam), qb)
    return o.transpose(1, 0, 2, 3, 4).reshape(b, t, h, ATTN_V_DIM)


def split_proj(p):
    b, t, _ = p.shape
    i0 = ATTN_QK_W
    i1 = i0 + ATTN_QK_W
    i2 = i1 + ATTN_V_W
    i3 = i2 + CONV_W
    i4 = i3 + CONV_W
    q = p[..., :i0].reshape(b, t, ATTN_HEADS, 2, ATTN_HEAD_DIM)
    k = p[..., i0:i1].reshape(b, t, ATTN_HEADS, 2, ATTN_HEAD_DIM)
    v = p[..., i1:i2].reshape(b, t, ATTN_HEADS, ATTN_V_DIM)
    return q, k, v, p[..., i2:i3], p[..., i3:i4], p[..., i4:]


def short_conv(gb, gc, xc, conv_w):
    u = gc * xc
    y = lax.conv_general_dilated(u, conv_w[:, None, :].astype(u.dtype), window_strides=(1,),
                                 padding=((CONV_K // 2, CONV_K // 2),),
                                 dimension_numbers=('NWC', 'WIO', 'NWC'),
                                 feature_group_count=CONV_W)
    return gb * y


def head_out(o, subln_g, lam_init):
    b, t = o.shape[:2]
    return (rmsnorm(o, subln_g) * (1.0 - lam_init)).reshape(b, t, ATTN_V_W)


def hybrid_mixer(xn, xn_c, w_in, conv_w, lq1, lk1, lq2, lk2, subln_g, w_out, layer, need_ctx):
    lam_init = 0.8 - 0.6 * math.exp(-0.3 * layer)
    lam = (jnp.exp(jnp.sum(lq1.astype(jnp.float32) * lk1.astype(jnp.float32)))
           - jnp.exp(jnp.sum(lq2.astype(jnp.float32) * lk2.astype(jnp.float32))) + lam_init)
    t = xn.shape[1]
    cos, sin = axial_rope_tables(t)
    q, k, v, gb, gc, xc = split_proj(jnp.einsum('btd,dn->btn', xn, w_in))
    qc, kc, vc, gbc, gcc, xcc = split_proj(jnp.einsum('btd,dn->btn', xn_c, w_in))
    q = apply_rope(q, cos, sin)
    k = apply_rope(k, cos, sin)
    k_all = jnp.concatenate([kc, k], axis=1)
    v_all = jnp.concatenate([vc, v], axis=1)
    attn = head_out(latent_diff_attention(q, k_all, v_all, lam), subln_g, lam_init)
    conv = short_conv(gb, gc, xc, conv_w)
    out = jnp.einsum('btn,nd->btd', jnp.concatenate([attn, conv], axis=-1), w_out)
    out_c = None
    if need_ctx:
        attn_c = head_out(diff_attend(qc, kc, vc, lam), subln_g, lam_init)
        conv_c = short_conv(gbc, gcc, xcc, conv_w)
        out_c = jnp.einsum('btn,nd->btd', jnp.concatenate([attn_c, conv_c], axis=-1), w_out)
    return out, out_c


def ec_moe(x_tok, w_router, w_gate, w_up, w_down):
    t = x_tok.shape[1]
    cap = CAP_FACTOR * t // N_EXPERTS
    logits = jnp.einsum('btd,de->bte', x_tok, w_router, preferred_element_type=jnp.float32)
    aff = jax.nn.softmax(logits, axis=-1)
    g, idx = lax.top_k(jnp.swapaxes(aff, 1, 2), cap)
    xs = jax.vmap(lambda xt, ix: xt[ix])(x_tok, idx)
    hg = jnp.einsum('becd,edf->becf', xs, w_gate)
    hu = jnp.einsum('becd,edf->becf', xs, w_up)
    y = jnp.einsum('becf,efd->becd', jax.nn.silu(hg) * hu, w_down) * g[..., None].astype(xs.dtype)
    return jax.vmap(lambda ix, val: jnp.zeros((t, D_MODEL), val.dtype)
                    .at[ix.reshape(-1)].add(val.reshape(-1, D_MODEL)))(idx, y)


def setup_inputs(seed: int = 0) -> dict:
    key = jax.random.key(seed)
    ks = jax.random.split(key, 24)
    f32 = jnp.float32
    nrm = lambda k, shape, s: jax.random.normal(k, shape, f32) * s
    gain = lambda k: 1.0 + 0.05 * jax.random.normal(k, (DEPTH, D_MODEL), f32)
    return {
        "x": nrm(ks[0], (BATCH, SEQ, D_MODEL), 1.0),
        "c": nrm(ks[1], (BATCH, D_MODEL), 1.0),
        "ctx": nrm(ks[2], (BATCH, CTX_LEN, D_MODEL), 1.0),
        "c_ctx": nrm(ks[3], (D_MODEL,), 1.0),
        "w_ada": nrm(ks[4], (DEPTH, D_MODEL, 6 * D_MODEL), 0.5 * D_MODEL ** -0.5),
        "b_ada": nrm(ks[5], (DEPTH, 6 * D_MODEL), 0.01),
        "norm_pre_mix": gain(ks[6]),
        "norm_post_mix": gain(ks[7]),
        "norm_pre_ffn": gain(ks[8]),
        "norm_post_ffn": gain(ks[9]),
        "w_in": nrm(ks[10], (DEPTH, D_MODEL, PROJ_W), D_MODEL ** -0.5),
        "conv_w": nrm(ks[11], (DEPTH, CONV_K, CONV_W), CONV_K ** -0.5),
        "lambda_q1": nrm(ks[12], (DEPTH, ATTN_HEAD_DIM), 0.1),
        "lambda_k1": nrm(ks[13], (DEPTH, ATTN_HEAD_DIM), 0.1),
        "lambda_q2": nrm(ks[14], (DEPTH, ATTN_HEAD_DIM), 0.1),
        "lambda_k2": nrm(ks[15], (DEPTH, ATTN_HEAD_DIM), 0.1),
        "subln_g": 1.0 + 0.05 * jax.random.normal(ks[16], (DEPTH, ATTN_V_DIM), f32),
        "w_out": nrm(ks[17], (DEPTH, MIX_W, D_MODEL), MIX_W ** -0.5),
        "w_router": nrm(ks[18], (DEPTH, D_MODEL, N_EXPERTS), D_MODEL ** -0.5),
        "w_gate": nrm(ks[19], (DEPTH, N_EXPERTS, D_MODEL, D_EXPERT), D_MODEL ** -0.5),
        "w_up": nrm(ks[20], (DEPTH, N_EXPERTS, D_MODEL, D_EXPERT), D_MODEL ** -0.5),
        "w_down": nrm(ks[21], (DEPTH, N_EXPERTS, D_EXPERT, D_MODEL), D_EXPERT ** -0.5),
    }


def reference(x, c, ctx, c_ctx, w_ada, b_ada, norm_pre_mix, norm_post_mix, norm_pre_ffn,
              norm_post_ffn, w_in, conv_w, lambda_q1, lambda_k1, lambda_q2, lambda_k2,
              subln_g, w_out, w_router, w_gate, w_up, w_down):
    h = x
    hc = ctx
    for layer in range(DEPTH):
        need_ctx = layer < DEPTH - 1
        mod = jnp.einsum('bd,dn->bn', jax.nn.silu(c), w_ada[layer]) + b_ada[layer]
        mod_c = jnp.einsum('d,dn->n', jax.nn.silu(c_ctx), w_ada[layer]) + b_ada[layer]
        sh1, sc1, g1, sh2, sc2, g2 = [m[:, None, :] for m in jnp.split(mod, 6, axis=-1)]
        csh1, csc1, cg1, csh2, csc2, cg2 = jnp.split(mod_c, 6, axis=-1)
        xn = modulate(rmsnorm(h, norm_pre_mix[layer]), sh1, sc1)
        xn_c = modulate(rmsnorm(hc, norm_pre_mix[layer]), csh1, csc1)
        mix, mix_c = hybrid_mixer(xn, xn_c, w_in[layer], conv_w[layer], lambda_q1[layer],
                                  lambda_k1[layer], lambda_q2[layer], lambda_k2[layer],
                                  subln_g[layer], w_out[layer], layer, need_ctx)
        h = h + g1 * rmsnorm(mix, norm_post_mix[layer])
        if need_ctx:
            hc = hc + cg1 * rmsnorm(mix_c, norm_post_mix[layer])
        xn = modulate(rmsnorm(h, norm_pre_ffn[layer]), sh2, sc2)
        ff = ec_moe(xn, w_router[layer], w_gate[layer], w_up[layer], w_down[layer])
        h = h + g2 * rmsnorm(ff, norm_post_ffn[layer])
        if need_ctx:
            xn_c = modulate(rmsnorm(hc, norm_pre_ffn[layer]), csh2, csc2)
            ff_c = ec_moe(xn_c, w_router[layer], w_gate[layer], w_up[layer], w_down[layer])
            hc = hc + cg2 * rmsnorm(ff_c, norm_post_ffn[layer])
    return h
```

```python
import functools
import math

import jax
import jax.numpy as jnp
from jax import lax
from jax.experimental import pallas as pl
from jax.experimental.pallas import tpu as pltpu

F32 = jnp.float32
BF16 = jnp.bfloat16
I32 = jnp.int32

D_MODEL = 1024
CTX_LEN = 256
GRID_W = 64
HEADS = 4
HEAD_DIM = 64
V_DIM = 2 * HEAD_DIM
QK_W = HEADS * 2 * HEAD_DIM
V_W = HEADS * V_DIM
CONV_W = 512
PROJ_W = 2 * QK_W + V_W + 3 * CONV_W
ROPE_BASE = 10000.0
ROT_AXIS = HEAD_DIM // 2
N_EXPERTS = 16
CAP_FACTOR = 2
EPS = 1e-6
LANES = 128
MOD_ROWS = 40

TM = 512
TQ = 256
FFN_G = 2
COMBINE_UNROLL = 8
VMEM_LIMIT = 48 * 1024 * 1024
HIGHEST = lax.Precision.HIGHEST
NT_DIMS = (((1,), (1,)), ((), ()))


def _params(*sem):
    return pltpu.CompilerParams(dimension_semantics=sem, vmem_limit_bytes=VMEM_LIMIT)


def _rms(x, g):
    return x * lax.rsqrt(jnp.mean(x * x, axis=-1, keepdims=True) + EPS) * g


def _silu(x):
    return x * jax.nn.sigmoid(x)


def _ada_kernel(cs_ref, w_ref, b_ref, o_ref):
    s = _silu(cs_ref[...])
    o_ref[...] = jnp.dot(s, w_ref[...], preferred_element_type=F32, precision=HIGHEST) + b_ref[...]


def _ada(cs, w, b):
    n = w.shape[1]
    tn = 512
    return pl.pallas_call(
        _ada_kernel,
        grid=(n // tn,),
        in_specs=[pl.BlockSpec((MOD_ROWS, D_MODEL), lambda j: (0, 0)),
                  pl.BlockSpec((D_MODEL, tn), lambda j: (0, j)),
                  pl.BlockSpec((1, tn), lambda j: (0, j))],
        out_specs=pl.BlockSpec((MOD_ROWS, tn), lambda j: (0, j)),
        out_shape=jax.ShapeDtypeStruct((MOD_ROWS, n), F32),
        compiler_params=_params("parallel"),
        name="ada",
    )(cs, w, b)


def _rope_store(p, tabs, out_ref):
    cos, sin_lo, sin_hi = tabs
    for j in range(QK_W // LANES):
        xj = p[:, j * LANES:(j + 1) * LANES]
        r = xj * cos + pltpu.roll(xj, LANES - 16, 1) * sin_lo + pltpu.roll(xj, 16, 1) * sin_hi
        out_ref[0, :, j * LANES:(j + 1) * LANES] = r.astype(BF16)


def _in_kernel(x_ref, xp_ref, xn_ref, mod_ref, g_ref, w_ref, cw_ref, tab_ref,
               q_ref, k_ref, v_ref, c_ref):
    i = pl.program_id(1)
    last = pl.num_programs(1) - 1
    g = g_ref[...]
    sh = mod_ref[0, :, 0:D_MODEL]
    sc = mod_ref[0, :, D_MODEL:2 * D_MODEL]

    def norm_mod(xt):
        return (_rms(xt, g) * (1.0 + sc) + sh).astype(BF16)

    def proj(xb, c0, width):
        return jnp.dot(xb, w_ref[:, c0:c0 + width], preferred_element_type=F32)

    xb = norm_mod(x_ref[0])
    tabs = [tab_ref[:, j * LANES:(j + 1) * LANES] for j in range(6)]
    _rope_store(proj(xb, 0, QK_W), tabs[0:3], q_ref)
    _rope_store(proj(xb, QK_W, QK_W), tabs[3:6], k_ref)
    v_ref[0] = proj(xb, 2 * QK_W, V_W).astype(BF16)

    c0 = 2 * QK_W + V_W
    gb = proj(xb, c0, CONV_W)
    u = proj(xb, c0 + CONV_W, CONV_W) * proj(xb, c0 + 2 * CONV_W, CONV_W)
    xh = norm_mod(jnp.concatenate([xp_ref[0], xn_ref[0]], axis=0))
    ph = proj(xh, c0 + CONV_W, 2 * CONV_W)
    uh = ph[:, :CONV_W] * ph[:, CONV_W:]
    u_before = jnp.where(i > 0, uh[7:8], 0.0)
    u_after = jnp.where(i < last, uh[8:9], 0.0)
    rows = lax.broadcasted_iota(I32, u.shape, 0)
    u_prev = jnp.where(rows == 0, u_before, pltpu.roll(u, 1, 0))
    u_next = jnp.where(rows == TM - 1, u_after, pltpu.roll(u, TM - 1, 0))
    cw = cw_ref[...]
    y = cw[0:1] * u_prev + cw[1:2] * u + cw[2:3] * u_next
    c_ref[0] = (gb * y).astype(BF16)


def _inproj(x, mod3, g, w_bf, conv_w, tab):
    b, t, _ = x.shape
    nt = t // TM
    hb = TM // 8
    out = jax.ShapeDtypeStruct((b, t, QK_W), BF16)
    tile = pl.BlockSpec((1, TM, QK_W), lambda bi, i: (bi, i, 0))
    return pl.pallas_call(
        _in_kernel,
        grid=(b, nt),
        in_specs=[
            pl.BlockSpec((1, TM, D_MODEL), lambda bi, i: (bi, i, 0)),
            pl.BlockSpec((1, 8, D_MODEL), lambda bi, i: (bi, jnp.maximum(i * hb - 1, 0), 0)),
            pl.BlockSpec((1, 8, D_MODEL), lambda bi, i: (bi, jnp.minimum((i + 1) * hb, t // 8 - 1), 0)),
            pl.BlockSpec((1, 1, 6 * D_MODEL), lambda bi, i: (bi, 0, 0)),
            pl.BlockSpec((1, D_MODEL), lambda bi, i: (0, 0)),
            pl.BlockSpec((D_MODEL, PROJ_W), lambda bi, i: (0, 0)),
            pl.BlockSpec((3, CONV_W), lambda bi, i: (0, 0)),
            pl.BlockSpec((TM, 6 * LANES), lambda bi, i: (i, 0)),
        ],
        out_specs=[tile, tile, tile, tile],
        out_shape=[out, out, out, out],
        compiler_params=_params("parallel", "parallel"),
        name="inproj",
    )(x, x, x, mod3, g, w_bf, conv_w, tab)


def _ctx_kernel(x_ref, mod_ref, g_ref, wk_ref, wv_ref, k_ref, v_ref):
    sh = mod_ref[0, :, 0:D_MODEL]
    sc = mod_ref[0, :, D_MODEL:2 * D_MODEL]
    xb = (_rms(x_ref[0], g_ref[...]) * (1.0 + sc) + sh).astype(BF16)
    k_ref[0] = jnp.dot(xb, wk_ref[...], preferred_element_type=F32).astype(BF16)
    v_ref[0] = jnp.dot(xb, wv_ref[...], preferred_element_type=F32).astype(BF16)


def _ctxproj(ctx, mod3, g, w_bf):
    b = ctx.shape[0]
    out = jax.ShapeDtypeStruct((b, CTX_LEN, QK_W), BF16)
    tile = pl.BlockSpec((1, CTX_LEN, QK_W), lambda bi: (bi, 0, 0))
    return pl.pallas_call(
        _ctx_kernel,
        grid=(b,),
        in_specs=[
            pl.BlockSpec((1, CTX_LEN, D_MODEL), lambda bi: (bi, 0, 0)),
            pl.BlockSpec((1, 1, 6 * D_MODEL), lambda bi: (b, 0, 0)),
            pl.BlockSpec((1, D_MODEL), lambda bi: (0, 0)),
            pl.BlockSpec((D_MODEL, QK_W), lambda bi: (0, 1)),
            pl.BlockSpec((D_MODEL, V_W), lambda bi: (0, 2)),
        ],
        out_specs=[tile, tile],
        out_shape=[out, out],
        compiler_params=_params("parallel"),
        name="ctxproj",
    )(ctx, mod3, g, w_bf, w_bf)


def _attn_kernel(q_ref, k_ref, v_ref, kc_ref, vc_ref, lq1_ref, lk1_ref, lq2_ref, lk2_ref,
                 sg_ref, o_ref, *, lam_init):
    lam = (jnp.exp(jnp.sum(lq1_ref[...] * lk1_ref[...], keepdims=True))
           - jnp.exp(jnp.sum(lq2_ref[...] * lk2_ref[...], keepdims=True)) + lam_init)
    lane = lax.broadcasted_iota(I32, (1, V_DIM), 1)
    first = (lane < HEAD_DIM).astype(BF16)
    second = (lane >= HEAD_DIM).astype(BF16)
    sg = sg_ref[...] * (1.0 - lam_init)

    for h in range(HEADS):
        sl = slice(h * V_DIM, (h + 1) * V_DIM)
        qh = q_ref[0, :, sl]
        kh = k_ref[0, :, sl]
        kch = kc_ref[0, :, sl]

        def softmax_parts(qm):
            s_l = lax.dot_general(qm, kh, NT_DIMS, preferred_element_type=F32)
            s_c = lax.dot_general(qm, kch, NT_DIMS, preferred_element_type=F32)
            m = jnp.maximum(jnp.max(s_l, axis=1, keepdims=True), jnp.max(s_c, axis=1, keepdims=True))
            p_l = jnp.exp(s_l - m)
            p_c = jnp.exp(s_c - m)
            den = jnp.sum(p_l, axis=1, keepdims=True) + jnp.sum(p_c, axis=1, keepdims=True)
            return p_l, p_c, den

        p0l, p0c, d0 = softmax_parts(qh * first)
        p1l, p1c, d1 = softmax_parts(qh * second)
        r0 = 1.0 / d0
        r1 = lam / d1
        a_l = (p0l * r0 - p1l * r1).astype(BF16)
        a_c = (p0c * r0 - p1c * r1).astype(BF16)
        o = (jnp.dot(a_l, v_ref[0, :, sl], preferred_element_type=F32)
             + jnp.dot(a_c, vc_ref[0, :, sl], preferred_element_type=F32))
        o_ref[0, :, sl] = _rms(o, sg).astype(BF16)


def _attn(q, k, v, kc, vc, lq1, lk1, lq2, lk2, sg, lam_init):
    b, t, _ = q.shape
    qt = pl.BlockSpec((1, TQ, QK_W), lambda bi, i: (bi, i, 0))
    full = pl.BlockSpec((1, t, QK_W), lambda bi, i: (bi, 0, 0))
    cfull = pl.BlockSpec((1, CTX_LEN, QK_W), lambda bi, i: (bi, 0, 0))
    vec = pl.BlockSpec((1, HEAD_DIM), lambda bi, i: (0, 0))
    return pl.pallas_call(
        functools.partial(_attn_kernel, lam_init=lam_init),
        grid=(b, t // TQ),
        in_specs=[qt, full, full, cfull, cfull, vec, vec, vec, vec,
                  pl.BlockSpec((1, V_DIM), lambda bi, i: (0, 0))],
        out_specs=qt,
        out_shape=jax.ShapeDtypeStruct((b, t, V_W), BF16),
        compiler_params=_params("parallel", "parallel"),
        name="attn",
    )(q, k, v, kc, vc, lq1, lk1, lq2, lk2, sg)


def _out_kernel(a_ref, c_ref, x_ref, mod_ref, gpost_ref, gpre_ref, wo_ref, wr_ref,
                h_ref, xn_ref, aff_ref):
    mix = (jnp.dot(a_ref[0], wo_ref[0:V_W, :], preferred_element_type=F32)
           + jnp.dot(c_ref[0], wo_ref[V_W:V_W + CONV_W, :], preferred_element_type=F32))
    g1 = mod_ref[0, :, 2 * D_MODEL:3 * D_MODEL]
    sh2 = mod_ref[0, :, 3 * D_MODEL:4 * D_MODEL]
    sc2 = mod_ref[0, :, 4 * D_MODEL:5 * D_MODEL]
    h1 = x_ref[0] + g1 * _rms(mix, gpost_ref[...])
    h_ref[0] = h1
    xn2 = _rms(h1, gpre_ref[...]) * (1.0 + sc2) + sh2
    xn_ref[0] = xn2
    logits = lax.dot_general(wr_ref[...], xn2, NT_DIMS, preferred_element_type=F32,
                             precision=HIGHEST)
    e = jnp.exp(logits - jnp.max(logits, axis=0, keepdims=True))
    aff_ref[0] = e / jnp.sum(e, axis=0, keepdims=True)


def _outproj(attn, conv, x, mod3, gpost, gpre, wo_bf, wr_t):
    b, t, _ = x.shape
    half = pl.BlockSpec((1, TM, V_W), lambda bi, i: (bi, i, 0))
    tile = pl.BlockSpec((1, TM, D_MODEL), lambda bi, i: (bi, i, 0))
    row = pl.BlockSpec((1, D_MODEL), lambda bi, i: (0, 0))
    act = jax.ShapeDtypeStruct((b, t, D_MODEL), F32)
    return pl.pallas_call(
        _out_kernel,
        grid=(b, t // TM),
        in_specs=[half, half, tile,
                  pl.BlockSpec((1, 1, 6 * D_MODEL), lambda bi, i: (bi, 0, 0)),
                  row, row,
                  pl.BlockSpec((V_W + CONV_W, D_MODEL), lambda bi, i: (0, 0)),
                  pl.BlockSpec((N_EXPERTS, D_MODEL), lambda bi, i: (0, 0))],
        out_specs=[tile, tile, pl.BlockSpec((1, N_EXPERTS, TM), lambda bi, i: (bi, 0, i))],
        out_shape=[act, act, jax.ShapeDtypeStruct((b, N_EXPERTS, t), F32)],
        compiler_params=_params("parallel", "parallel"),
        name="outproj",
    )(attn, conv, x, mod3, gpost, gpre, wo_bf, wr_t)


def _prefix_exclusive(mask_f):
    r = lax.broadcasted_iota(I32, (LANES, LANES), 0)
    c = lax.broadcasted_iota(I32, (LANES, LANES), 1)
    upper = (r <= c).astype(BF16)
    off = jnp.zeros((mask_f.shape[0], 1), F32)
    outs = []
    for j in range(mask_f.shape[1] // LANES):
        blk = mask_f[:, j * LANES:(j + 1) * LANES]
        inc = jnp.dot(blk.astype(BF16), upper, preferred_element_type=F32)
        outs.append(inc - blk + off)
        off = off + inc[:, LANES - 1:LANES]
    return jnp.concatenate(outs, axis=1)


def _topk_kernel(aff_ref, idx_ref, gate_ref, *, cap):
    aff = aff_ref[0]
    n_e, t = aff.shape

    def search(i, thr_bits):
        cand = thr_bits | (jnp.int32(1) << (29 - i))
        cnt = jnp.sum((aff >= pltpu.bitcast(cand, F32)).astype(F32), axis=1, keepdims=True)
        return jnp.where(cnt >= cap, cand, thr_bits)

    thr = pltpu.bitcast(lax.fori_loop(0, 30, search, jnp.zeros((n_e, 1), I32)), F32)
    gt = aff > thr
    eq = aff == thr
    need = cap - jnp.sum(gt.astype(F32), axis=1, keepdims=True)
    eq_f = eq.astype(F32)
    take_eq = jnp.where(_prefix_exclusive(eq_f) < need, eq_f, 0.0)
    sel_f = jnp.where(gt, 1.0, take_eq)
    slot = jnp.where(sel_f > 0.0, _prefix_exclusive(sel_f), -1.0)

    tok = lax.broadcasted_iota(I32, (cap, t), 1).astype(F32)
    slot_ids = lax.broadcasted_iota(I32, (cap, 1), 0).astype(F32)
    lane = lax.broadcasted_iota(I32, (cap, LANES), 1)
    idx_acc = jnp.zeros((cap, LANES), F32)
    gate_acc = jnp.zeros((cap, LANES), F32)
    for e in range(n_e):
        hit = slot[e:e + 1, :] == slot_ids
        idx_col = jnp.sum(jnp.where(hit, tok, 0.0), axis=1, keepdims=True)
        gate_col = jnp.sum(jnp.where(hit, aff[e:e + 1, :], 0.0), axis=1, keepdims=True)
        idx_acc = jnp.where(lane == e, idx_col, idx_acc)
        gate_acc = jnp.where(lane == e, gate_col, gate_acc)
    idx_ref[0] = idx_acc.astype(I32)
    gate_ref[0] = gate_acc


def _topk(aff_t, cap):
    b, n_e, t = aff_t.shape
    out = pl.BlockSpec((1, cap, LANES), lambda bi: (bi, 0, 0))
    return pl.pallas_call(
        functools.partial(_topk_kernel, cap=cap),
        grid=(b,),
        in_specs=[pl.BlockSpec((1, n_e, t), lambda bi: (bi, 0, 0))],
        out_specs=[out, out],
        out_shape=[jax.ShapeDtypeStruct((b, cap, LANES), I32),
                   jax.ShapeDtypeStruct((b, cap, LANES), F32)],
        compiler_params=_params("parallel"),
        name="topk",
    )(aff_t)


def _ffn_kernel(idx_ref, idx_next_ref, xn_hbm, gate_ref, wg_ref, wu_ref, wd_ref, y_ref,
                xs_a, xs_b, sems, *, cap):
    nb = pl.num_programs(1)
    step = pl.program_id(0) * nb + pl.program_id(1)
    last = pl.num_programs(0) * nb - 1
    rows = FFN_G * cap
    bufs = (xs_a, xs_b)

    def row_copy(ids_ref, batch0, g, r, half):
        tkn = ids_ref[half * FFN_G + g, 0, 0, r]
        return pltpu.make_async_copy(xn_hbm.at[batch0 + half * FFN_G + g, pl.ds(tkn, 1), :],
                                     bufs[half].at[pl.ds(g * cap + r, 1), :], sems.at[half])

    def wait_half(half):
        pltpu.make_async_copy(xn_hbm.at[0, pl.ds(0, rows), :], bufs[half], sems.at[half]).wait()

    def expert_up(half):
        xs = bufs[half][...].astype(BF16)
        hg = jnp.dot(xs, wg_ref[0], preferred_element_type=F32)
        hu = jnp.dot(xs, wu_ref[0], preferred_element_type=F32)
        return (_silu(hg) * hu).astype(BF16)

    def expert_down(half, h):
        y = jnp.dot(h, wd_ref[0], preferred_element_type=F32)
        for g in range(FFN_G):
            k = half * FFN_G + g
            y_ref[k, 0] = y[g * cap:(g + 1) * cap] * gate_ref[k, 0]

    batch0 = pl.program_id(1) * (2 * FFN_G)

    @pl.when(step == 0)
    def _():
        for g in range(FFN_G):
            def issue(r, carry, g=g):
                row_copy(idx_ref, batch0, g, r, 0).start()
                return carry
            lax.fori_loop(0, cap, issue, 0)
        wait_half(0)

    for g in range(FFN_G):
        for r in range(cap):
            row_copy(idx_ref, batch0, g, r, 1).start()
    h0 = expert_up(0)
    wait_half(1)
    expert_down(0, h0)

    nxt_batch0 = (jnp.minimum(step + 1, last) % nb) * (2 * FFN_G)
    for g in range(FFN_G):
        for r in range(cap):
            row_copy(idx_next_ref, nxt_batch0, g, r, 0).start()
    h1 = expert_up(1)
    wait_half(0)
    expert_down(1, h1)


def _ffn(idx4, xn2, gate4, wg, wu, wd, cap):
    b, t, d = xn2.shape
    n_e = wg.shape[0]
    blk = 2 * FFN_G
    nb = b // blk
    wspec = pl.BlockSpec((1, d, d), lambda e, bg: (e, 0, 0))

    def next_ids(e, bg):
        nxt = jnp.minimum(e * nb + bg + 1, n_e * nb - 1)
        return (nxt % nb, nxt // nb, 0, 0)

    return pl.pallas_call(
        functools.partial(_ffn_kernel, cap=cap),
        grid=(n_e, nb),
        in_specs=[
            pl.BlockSpec((blk, 1, 1, cap), lambda e, bg: (bg, e, 0, 0), memory_space=pltpu.SMEM),
            pl.BlockSpec((blk, 1, 1, cap), next_ids, memory_space=pltpu.SMEM),
            pl.BlockSpec(memory_space=pl.ANY),
            pl.BlockSpec((blk, 1, cap, 1), lambda e, bg: (bg, e, 0, 0)),
            wspec, wspec, wspec,
        ],
        out_specs=pl.BlockSpec((blk, 1, cap, d), lambda e, bg: (bg, e, 0, 0)),
        out_shape=jax.ShapeDtypeStruct((b, n_e, cap, d), F32),
        scratch_shapes=[pltpu.VMEM((FFN_G * cap, d), F32), pltpu.VMEM((FFN_G * cap, d), F32),
                        pltpu.SemaphoreType.DMA((2,))],
        compiler_params=_params("arbitrary", "arbitrary"),
        name="ffn",
    )(idx4, idx4, xn2, gate4, wg, wu, wd)


def _combine_kernel(idx_ref, y_ref, ff_ref, *, cap):
    e = pl.program_id(1)

    @pl.when(e == 0)
    def _():
        ff_ref[...] = jnp.zeros_like(ff_ref)

    def add_rows(j, carry):
        r0 = j * COMBINE_UNROLL
        toks = [idx_ref[0, 0, 0, r0 + u] for u in range(COMBINE_UNROLL)]
        sums = [ff_ref[0, pl.ds(toks[u], 1), :] + y_ref[0, 0, pl.ds(r0 + u, 1), :]
                for u in range(COMBINE_UNROLL)]
        for u in range(COMBINE_UNROLL):
            ff_ref[0, pl.ds(toks[u], 1), :] = sums[u]
        return carry
    lax.fori_loop(0, cap // COMBINE_UNROLL, add_rows, 0)


def _combine(idx4, y, t, cap):
    b, n_e, _, d = y.shape
    return pl.pallas_call(
        functools.partial(_combine_kernel, cap=cap),
        grid=(b, n_e),
        in_specs=[
            pl.BlockSpec((1, 1, 1, cap), lambda bi, e: (bi, e, 0, 0), memory_space=pltpu.SMEM),
            pl.BlockSpec((1, 1, cap, d), lambda bi, e: (bi, e, 0, 0)),
        ],
        out_specs=pl.BlockSpec((1, t, d), lambda bi, e: (bi, 0, 0)),
        out_shape=jax.ShapeDtypeStruct((b, t, d), F32),
        compiler_params=_params("parallel", "arbitrary"),
        name="combine",
    )(idx4, y)


def _final_kernel(h_ref, ff_ref, mod_ref, g_ref, o_ref):
    g2 = mod_ref[0, :, 5 * D_MODEL:6 * D_MODEL]
    o_ref[0] = h_ref[0] + g2 * _rms(ff_ref[0], g_ref[...])


def _final(h1, ff, mod3, g):
    b, t, d = h1.shape
    tile = pl.BlockSpec((1, TM, d), lambda bi, i: (bi, i, 0))
    return pl.pallas_call(
        _final_kernel,
        grid=(b, t // TM),
        in_specs=[tile, tile,
                  pl.BlockSpec((1, 1, 6 * d), lambda bi, i: (bi, 0, 0)),
                  pl.BlockSpec((1, d), lambda bi, i: (0, 0))],
        out_specs=tile,
        out_shape=jax.ShapeDtypeStruct((b, t, d), F32),
        compiler_params=_params("parallel", "parallel"),
        name="final",
    )(h1, ff, mod3, g)


def _rope_tables(t):
    rows = t // GRID_W
    row = jnp.repeat(jnp.arange(rows, dtype=F32), GRID_W)
    col = jnp.tile(jnp.arange(GRID_W, dtype=F32), rows)
    inv = ROPE_BASE ** (-jnp.arange(0, ROT_AXIS, 2, dtype=F32) / ROT_AXIS)
    ang_r = row[:, None] * inv[None, :]
    ang_c = col[:, None] * inv[None, :]
    ang = jnp.concatenate([ang_r, ang_r, ang_c, ang_c], axis=-1)
    ang = jnp.concatenate([ang, ang], axis=-1)
    cos, sin = jnp.cos(ang), jnp.sin(ang)
    low = (jnp.arange(LANES) % ROT_AXIS) < ROT_AXIS // 2
    sin_lo = jnp.where(low[None, :], -sin, 0.0)
    sin_hi = jnp.where(low[None, :], 0.0, sin)
    scale = 1.0 / math.sqrt(HEAD_DIM)
    return jnp.concatenate([cos * scale, sin_lo * scale, sin_hi * scale, cos, sin_lo, sin_hi], axis=-1)


def kernel(x, c, ctx, c_ctx, w_ada, b_ada, norm_pre_mix, norm_post_mix, norm_pre_ffn, norm_post_ffn,
           w_in, conv_w, lambda_q1, lambda_k1, lambda_q2, lambda_k2, subln_g, w_out, w_router,
           w_gate, w_up, w_down):
    b, t, d = x.shape
    depth = w_ada.shape[0]
    assert depth == 1 and d == D_MODEL and t % TM == 0 and t % TQ == 0 and b % (2 * FFN_G) == 0
    assert b + 1 <= MOD_ROWS
    layer = 0
    lam_init = 0.8 - 0.6 * math.exp(-0.3 * layer)
    cap = CAP_FACTOR * t // N_EXPERTS

    cs = jnp.concatenate([c, c_ctx[None, :], jnp.zeros((MOD_ROWS - b - 1, d), F32)], axis=0)
    mod3 = _ada(cs, w_ada[layer], b_ada[layer][None, :]).reshape(MOD_ROWS, 1, 6 * d)

    w_in_bf = w_in[layer].astype(BF16)
    q, k, v, conv = _inproj(x, mod3, norm_pre_mix[layer][None, :], w_in_bf, conv_w[layer], _rope_tables(t))
    kc, vc = _ctxproj(ctx, mod3, norm_pre_mix[layer][None, :], w_in_bf)
    attn = _attn(q, k, v, kc, vc, lambda_q1[layer][None, :], lambda_k1[layer][None, :],
                 lambda_q2[layer][None, :], lambda_k2[layer][None, :], subln_g[layer][None, :], lam_init)
    h1, xn2, aff_t = _outproj(attn, conv, x, mod3, norm_post_mix[layer][None, :],
                              norm_pre_ffn[layer][None, :], w_out[layer].astype(BF16), w_router[layer].T)

    idx_p, gate_p = _topk(aff_t, cap)
    idx = jnp.swapaxes(idx_p[:, :, :N_EXPERTS], 1, 2)
    gate = jnp.swapaxes(gate_p[:, :, :N_EXPERTS], 1, 2)
    idx4 = idx.reshape(b, N_EXPERTS, 1, cap)
    gate4 = gate.reshape(b, N_EXPERTS, cap, 1)

    y = _ffn(idx4, xn2, gate4, w_gate[layer].astype(BF16), w_up[layer].astype(BF16),
             w_down[layer].astype(BF16), cap)
    ff = _combine(idx4, y, t, cap)
    return _final(h1, ff, mod3, norm_post_ffn[layer][None, :])
```

```python
import functools
import math

import jax
import jax.numpy as jnp
from jax import lax
from jax.experimental import pallas as pl
from jax.experimental.pallas import tpu as pltpu

F32 = jnp.float32
BF16 = jnp.bfloat16
I32 = jnp.int32

D_MODEL = 1024
CTX_LEN = 256
GRID_W = 64
HEADS = 4
HEAD_DIM = 64
V_DIM = 2 * HEAD_DIM
QK_W = HEADS * 2 * HEAD_DIM
V_W = HEADS * V_DIM
CONV_W = 512
PROJ_W = 2 * QK_W + V_W + 3 * CONV_W
ROPE_BASE = 10000.0
ROT_AXIS = HEAD_DIM // 2
N_EXPERTS = 16
CAP_FACTOR = 2
EPS = 1e-6
LANES = 128
MOD_ROWS = 40

TM = 512
TQ = 256
TQ_BLOCK = 512
FFN_G = 2
COMBINE_UNROLL = 8
VMEM_LIMIT = 48 * 1024 * 1024
HIGHEST = lax.Precision.HIGHEST
NT_DIMS = (((1,), (1,)), ((), ()))


def _params(*sem):
    return pltpu.CompilerParams(dimension_semantics=sem, vmem_limit_bytes=VMEM_LIMIT)


def _rms(x, g):
    return x * lax.rsqrt(jnp.mean(x * x, axis=-1, keepdims=True) + EPS) * g


def _silu(x):
    return x * jax.nn.sigmoid(x)


def _ada_kernel(cs_ref, w_ref, b_ref, o_ref):
    s = _silu(cs_ref[...])
    o_ref[...] = jnp.dot(s, w_ref[...], preferred_element_type=F32, precision=HIGHEST) + b_ref[...]


def _ada(cs, w, b):
    n = w.shape[1]
    tn = 512
    return pl.pallas_call(
        _ada_kernel,
        grid=(n // tn,),
        in_specs=[pl.BlockSpec((MOD_ROWS, D_MODEL), lambda j: (0, 0)),
                  pl.BlockSpec((D_MODEL, tn), lambda j: (0, j)),
                  pl.BlockSpec((1, tn), lambda j: (0, j))],
        out_specs=pl.BlockSpec((MOD_ROWS, tn), lambda j: (0, j)),
        out_shape=jax.ShapeDtypeStruct((MOD_ROWS, n), F32),
        compiler_params=_params("parallel"),
        name="ada",
    )(cs, w, b)


def _rope_store(p, tabs, out_ref):
    cos, sin_lo, sin_hi = tabs
    for j in range(QK_W // LANES):
        xj = p[:, j * LANES:(j + 1) * LANES]
        r = xj * cos + pltpu.roll(xj, LANES - 16, 1) * sin_lo + pltpu.roll(xj, 16, 1) * sin_hi
        out_ref[0, :, j * LANES:(j + 1) * LANES] = r.astype(BF16)


def _store_values(v_ref, pv):
    ones = jnp.ones((pv.shape[0], V_DIM), BF16)
    for h in range(HEADS):
        v_ref[0, :, 2 * h * V_DIM:(2 * h + 1) * V_DIM] = pv[:, h * V_DIM:(h + 1) * V_DIM].astype(BF16)
        v_ref[0, :, (2 * h + 1) * V_DIM:(2 * h + 2) * V_DIM] = ones


def _in_kernel(x_ref, xp_ref, xn_ref, mod_ref, g_ref, w_ref, cw_ref, tab_ref,
               q_ref, k_ref, v_ref, c_ref):
    i = pl.program_id(1)
    last = pl.num_programs(1) - 1
    g = g_ref[...]
    sh = mod_ref[0, :, 0:D_MODEL]
    sc = mod_ref[0, :, D_MODEL:2 * D_MODEL]

    def norm_mod(xt):
        return (_rms(xt, g) * (1.0 + sc) + sh).astype(BF16)

    def proj(xb, c0, width):
        return jnp.dot(xb, w_ref[:, c0:c0 + width], preferred_element_type=F32)

    xb = norm_mod(x_ref[0])
    tabs = [tab_ref[:, j * LANES:(j + 1) * LANES] for j in range(6)]
    _rope_store(proj(xb, 0, QK_W), tabs[0:3], q_ref)
    _rope_store(proj(xb, QK_W, QK_W), tabs[3:6], k_ref)
    _store_values(v_ref, proj(xb, 2 * QK_W, V_W))

    c0 = 2 * QK_W + V_W
    gb = proj(xb, c0, CONV_W)
    u = proj(xb, c0 + CONV_W, CONV_W) * proj(xb, c0 + 2 * CONV_W, CONV_W)
    xh = norm_mod(jnp.concatenate([xp_ref[0], xn_ref[0]], axis=0))
    ph = proj(xh, c0 + CONV_W, 2 * CONV_W)
    uh = ph[:, :CONV_W] * ph[:, CONV_W:]
    u_before = jnp.where(i > 0, uh[7:8], 0.0)
    u_after = jnp.where(i < last, uh[8:9], 0.0)
    rows = lax.broadcasted_iota(I32, u.shape, 0)
    u_prev = jnp.where(rows == 0, u_before, pltpu.roll(u, 1, 0))
    u_next = jnp.where(rows == TM - 1, u_after, pltpu.roll(u, TM - 1, 0))
    cw = cw_ref[...]
    y = cw[0:1] * u_prev + cw[1:2] * u + cw[2:3] * u_next
    c_ref[0] = (gb * y).astype(BF16)


def _inproj(x, mod3, g, w_bf, conv_w, tab):
    b, t, _ = x.shape
    nt = t // TM
    hb = TM // 8
    out = jax.ShapeDtypeStruct((b, t, QK_W), BF16)
    tile = pl.BlockSpec((1, TM, QK_W), lambda bi, i: (bi, i, 0))
    return pl.pallas_call(
        _in_kernel,
        grid=(b, nt),
        in_specs=[
            pl.BlockSpec((1, TM, D_MODEL), lambda bi, i: (bi, i, 0)),
            pl.BlockSpec((1, 8, D_MODEL), lambda bi, i: (bi, jnp.maximum(i * hb - 1, 0), 0)),
            pl.BlockSpec((1, 8, D_MODEL), lambda bi, i: (bi, jnp.minimum((i + 1) * hb, t // 8 - 1), 0)),
            pl.BlockSpec((1, 1, 6 * D_MODEL), lambda bi, i: (bi, 0, 0)),
            pl.BlockSpec((1, D_MODEL), lambda bi, i: (0, 0)),
            pl.BlockSpec((D_MODEL, PROJ_W), lambda bi, i: (0, 0)),
            pl.BlockSpec((3, CONV_W), lambda bi, i: (0, 0)),
            pl.BlockSpec((TM, 6 * LANES), lambda bi, i: (i, 0)),
        ],
        out_specs=[tile, tile, pl.BlockSpec((1, TM, 2 * V_W), lambda bi, i: (bi, i, 0)), tile],
        out_shape=[out, out, jax.ShapeDtypeStruct((b, t, 2 * V_W), BF16), out],
        compiler_params=_params("parallel", "parallel"),
        name="inproj",
    )(x, x, x, mod3, g, w_bf, conv_w, tab)


def _ctx_kernel(x_ref, mod_ref, g_ref, wk_ref, wv_ref, k_ref, v_ref):
    sh = mod_ref[0, :, 0:D_MODEL]
    sc = mod_ref[0, :, D_MODEL:2 * D_MODEL]
    xb = (_rms(x_ref[0], g_ref[...]) * (1.0 + sc) + sh).astype(BF16)
    k_ref[0] = jnp.dot(xb, wk_ref[...], preferred_element_type=F32).astype(BF16)
    _store_values(v_ref, jnp.dot(xb, wv_ref[...], preferred_element_type=F32))


def _ctxproj(ctx, mod3, g, w_bf):
    b = ctx.shape[0]
    out = jax.ShapeDtypeStruct((b, CTX_LEN, QK_W), BF16)
    tile = pl.BlockSpec((1, CTX_LEN, QK_W), lambda bi: (bi, 0, 0))
    return pl.pallas_call(
        _ctx_kernel,
        grid=(b,),
        in_specs=[
            pl.BlockSpec((1, CTX_LEN, D_MODEL), lambda bi: (bi, 0, 0)),
            pl.BlockSpec((1, 1, 6 * D_MODEL), lambda bi: (b, 0, 0)),
            pl.BlockSpec((1, D_MODEL), lambda bi: (0, 0)),
            pl.BlockSpec((D_MODEL, QK_W), lambda bi: (0, 1)),
            pl.BlockSpec((D_MODEL, V_W), lambda bi: (0, 2)),
        ],
        out_specs=[tile, pl.BlockSpec((1, CTX_LEN, 2 * V_W), lambda bi: (bi, 0, 0))],
        out_shape=[out, jax.ShapeDtypeStruct((b, CTX_LEN, 2 * V_W), BF16)],
        compiler_params=_params("parallel"),
        name="ctxproj",
    )(ctx, mod3, g, w_bf, w_bf)


def _attn_kernel(q_ref, k_ref, v_ref, kc_ref, vc_ref, lq1_ref, lk1_ref, lq2_ref, lk2_ref,
                 sg_ref, o_ref, *, lam_init):
    lam = (jnp.exp(jnp.sum(lq1_ref[...] * lk1_ref[...], keepdims=True))
           - jnp.exp(jnp.sum(lq2_ref[...] * lk2_ref[...], keepdims=True)) + lam_init)
    lane = lax.broadcasted_iota(I32, (1, V_DIM), 1)
    first = (lane < HEAD_DIM).astype(BF16)
    second = (lane >= HEAD_DIM).astype(BF16)
    sg = sg_ref[...] * (1.0 - lam_init)

    for h, u in [(h, u) for u in range(TQ_BLOCK // TQ) for h in range(HEADS)]:
        sl = slice(h * V_DIM, (h + 1) * V_DIM)
        sv = slice(h * 2 * V_DIM, (h + 1) * 2 * V_DIM)
        rows = slice(u * TQ, (u + 1) * TQ)
        qh = q_ref[0, rows, sl]
        q2 = jnp.concatenate([qh * first, qh * second], axis=0)
        s_l = lax.dot_general(q2, k_ref[0, :, sl], NT_DIMS, preferred_element_type=F32)
        s_c = lax.dot_general(q2, kc_ref[0, :, sl], NT_DIMS, preferred_element_type=F32)
        m = jnp.maximum(jnp.max(s_l, axis=1, keepdims=True), jnp.max(s_c, axis=1, keepdims=True))
        p_l = jnp.exp2(s_l - m).astype(BF16)
        p_c = jnp.exp2(s_c - m).astype(BF16)
        pv = (jnp.dot(p_l, v_ref[0, :, sv], preferred_element_type=F32)
              + jnp.dot(p_c, vc_ref[0, :, sv], preferred_element_type=F32))
        num = pv[:, :V_DIM]
        den = pv[:, V_DIM:V_DIM + 1]
        o = num[:TQ] * (1.0 / den[:TQ]) - num[TQ:] * (lam / den[TQ:])
        o_ref[0, rows, sl] = _rms(o, sg).astype(BF16)


def _attn(q, k, v, kc, vc, lq1, lk1, lq2, lk2, sg, lam_init):
    b, t, _ = q.shape
    qt = pl.BlockSpec((1, TQ_BLOCK, QK_W), lambda bi, i: (bi, i, 0))
    vec = pl.BlockSpec((1, HEAD_DIM), lambda bi, i: (0, 0))

    def per_batch(rows, width):
        return pl.BlockSpec((1, rows, width), lambda bi, i: (bi, 0, 0))

    return pl.pallas_call(
        functools.partial(_attn_kernel, lam_init=lam_init),
        grid=(b, t // TQ_BLOCK),
        in_specs=[qt, per_batch(t, QK_W), per_batch(t, 2 * V_W),
                  per_batch(CTX_LEN, QK_W), per_batch(CTX_LEN, 2 * V_W),
                  vec, vec, vec, vec, pl.BlockSpec((1, V_DIM), lambda bi, i: (0, 0))],
        out_specs=qt,
        out_shape=jax.ShapeDtypeStruct((b, t, V_W), BF16),
        compiler_params=_params("parallel", "parallel"),
        name="attn",
    )(q, k, v, kc, vc, lq1, lk1, lq2, lk2, sg)


def _out_kernel(a_ref, c_ref, x_ref, mod_ref, gpost_ref, gpre_ref, wo_ref, wr_ref,
                h_ref, xn_ref, aff_ref):
    mix = (jnp.dot(a_ref[0], wo_ref[0:V_W, :], preferred_element_type=F32)
           + jnp.dot(c_ref[0], wo_ref[V_W:V_W + CONV_W, :], preferred_element_type=F32))
    g1 = mod_ref[0, :, 2 * D_MODEL:3 * D_MODEL]
    sh2 = mod_ref[0, :, 3 * D_MODEL:4 * D_MODEL]
    sc2 = mod_ref[0, :, 4 * D_MODEL:5 * D_MODEL]
    h1 = x_ref[0] + g1 * _rms(mix, gpost_ref[...])
    h_ref[0] = h1
    xn2 = _rms(h1, gpre_ref[...]) * (1.0 + sc2) + sh2
    xn_ref[0] = xn2
    logits = lax.dot_general(wr_ref[...], xn2, NT_DIMS, preferred_element_type=F32,
                             precision=HIGHEST)
    e = jnp.exp(logits - jnp.max(logits, axis=0, keepdims=True))
    aff_ref[0] = e / jnp.sum(e, axis=0, keepdims=True)


def _outproj(attn, conv, x, mod3, gpost, gpre, wo_bf, wr_t):
    b, t, _ = x.shape
    half = pl.BlockSpec((1, TM, V_W), lambda bi, i: (bi, i, 0))
    tile = pl.BlockSpec((1, TM, D_MODEL), lambda bi, i: (bi, i, 0))
    row = pl.BlockSpec((1, D_MODEL), lambda bi, i: (0, 0))
    act = jax.ShapeDtypeStruct((b, t, D_MODEL), F32)
    return pl.pallas_call(
        _out_kernel,
        grid=(b, t // TM),
        in_specs=[half, half, tile,
                  pl.BlockSpec((1, 1, 6 * D_MODEL), lambda bi, i: (bi, 0, 0)),
                  row, row,
                  pl.BlockSpec((V_W + CONV_W, D_MODEL), lambda bi, i: (0, 0)),
                  pl.BlockSpec((N_EXPERTS, D_MODEL), lambda bi, i: (0, 0))],
        out_specs=[tile, tile, pl.BlockSpec((1, N_EXPERTS, TM), lambda bi, i: (bi, 0, i))],
        out_shape=[act, act, jax.ShapeDtypeStruct((b, N_EXPERTS, t), F32)],
        compiler_params=_params("parallel", "parallel"),
        name="outproj",
    )(attn, conv, x, mod3, gpost, gpre, wo_bf, wr_t)


def _prefix_exclusive(mask_f):
    r = lax.broadcasted_iota(I32, (LANES, LANES), 0)
    c = lax.broadcasted_iota(I32, (LANES, LANES), 1)
    upper = (r <= c).astype(BF16)
    off = jnp.zeros((mask_f.shape[0], 1), F32)
    outs = []
    for j in range(mask_f.shape[1] // LANES):
        blk = mask_f[:, j * LANES:(j + 1) * LANES]
        inc = jnp.dot(blk.astype(BF16), upper, preferred_element_type=F32)
        outs.append(inc - blk + off)
        off = off + inc[:, LANES - 1:LANES]
    return jnp.concatenate(outs, axis=1)


def _topk_kernel(aff_ref, idx_ref, gate_ref, *, cap):
    aff = aff_ref[0]
    n_e, t = aff.shape

    def search(i, thr_bits):
        cand = thr_bits | (jnp.int32(1) << (29 - i))
        cnt = jnp.sum((aff >= pltpu.bitcast(cand, F32)).astype(F32), axis=1, keepdims=True)
        return jnp.where(cnt >= cap, cand, thr_bits)

    thr = pltpu.bitcast(lax.fori_loop(0, 30, search, jnp.zeros((n_e, 1), I32)), F32)
    gt = aff > thr
    eq = aff == thr
    need = cap - jnp.sum(gt.astype(F32), axis=1, keepdims=True)
    eq_f = eq.astype(F32)
    take_eq = jnp.where(_prefix_exclusive(eq_f) < need, eq_f, 0.0)
    sel_f = jnp.where(gt, 1.0, take_eq)
    slot = jnp.where(sel_f > 0.0, _prefix_exclusive(sel_f), -1.0)

    tok = lax.broadcasted_iota(I32, (cap, t), 1).astype(F32)
    slot_ids = lax.broadcasted_iota(I32, (cap, 1), 0).astype(F32)
    lane = lax.broadcasted_iota(I32, (cap, LANES), 1)
    idx_acc = jnp.zeros((cap, LANES), F32)
    gate_acc = jnp.zeros((cap, LANES), F32)
    for e in range(n_e):
        hit = slot[e:e + 1, :] == slot_ids
        idx_col = jnp.sum(jnp.where(hit, tok, 0.0), axis=1, keepdims=True)
        gate_col = jnp.sum(jnp.where(hit, aff[e:e + 1, :], 0.0), axis=1, keepdims=True)
        idx_acc = jnp.where(lane == e, idx_col, idx_acc)
        gate_acc = jnp.where(lane == e, gate_col, gate_acc)
    idx_ref[0] = idx_acc.astype(I32)
    gate_ref[0] = gate_acc


def _topk(aff_t, cap):
    b, n_e, t = aff_t.shape
    out = pl.BlockSpec((1, cap, LANES), lambda bi: (bi, 0, 0))
    return pl.pallas_call(
        functools.partial(_topk_kernel, cap=cap),
        grid=(b,),
        in_specs=[pl.BlockSpec((1, n_e, t), lambda bi: (bi, 0, 0))],
        out_specs=[out, out],
        out_shape=[jax.ShapeDtypeStruct((b, cap, LANES), I32),
                   jax.ShapeDtypeStruct((b, cap, LANES), F32)],
        compiler_params=_params("parallel"),
        name="topk",
    )(aff_t)


def _ffn_kernel(idx_ref, idx_next_ref, xn_hbm, gate_ref, wg_ref, wu_ref, wd_ref, y_ref,
                xs_a, xs_b, sems, *, cap):
    nb = pl.num_programs(1)
    step = pl.program_id(0) * nb + pl.program_id(1)
    last = pl.num_programs(0) * nb - 1
    rows = FFN_G * cap
    bufs = (xs_a, xs_b)

    def row_copy(ids_ref, batch0, g, r, half):
        tkn = ids_ref[half * FFN_G + g, 0, 0, r]
        return pltpu.make_async_copy(xn_hbm.at[batch0 + half * FFN_G + g, pl.ds(tkn, 1), :],
                                     bufs[half].at[pl.ds(g * cap + r, 1), :], sems.at[half])

    def wait_half(half):
        pltpu.make_async_copy(xn_hbm.at[0, pl.ds(0, rows), :], bufs[half], sems.at[half]).wait()

    def expert_up(half):
        xs = bufs[half][...].astype(BF16)
        hg = jnp.dot(xs, wg_ref[0], preferred_element_type=F32)
        hu = jnp.dot(xs, wu_ref[0], preferred_element_type=F32)
        return (_silu(hg) * hu).astype(BF16)

    def expert_down(half, h):
        y = jnp.dot(h, wd_ref[0], preferred_element_type=F32)
        for g in range(FFN_G):
            k = half * FFN_G + g
            y_ref[k, 0] = y[g * cap:(g + 1) * cap] * gate_ref[k, 0]

    batch0 = pl.program_id(1) * (2 * FFN_G)

    @pl.when(step == 0)
    def _():
        for g in range(FFN_G):
            def issue(r, carry, g=g):
                row_copy(idx_ref, batch0, g, r, 0).start()
                return carry
            lax.fori_loop(0, cap, issue, 0)
        wait_half(0)

    for g in range(FFN_G):
        for r in range(cap):
            row_copy(idx_ref, batch0, g, r, 1).start()
    h0 = expert_up(0)
    wait_half(1)
    expert_down(0, h0)

    nxt_batch0 = (jnp.minimum(step + 1, last) % nb) * (2 * FFN_G)
    for g in range(FFN_G):
        for r in range(cap):
            row_copy(idx_next_ref, nxt_batch0, g, r, 0).start()
    h1 = expert_up(1)
    wait_half(0)
    expert_down(1, h1)


def _ffn(idx4, xn2, gate4, wg, wu, wd, cap):
    b, t, d = xn2.shape
    n_e = wg.shape[0]
    blk = 2 * FFN_G
    nb = b // blk
    wspec = pl.BlockSpec((1, d, d), lambda e, bg: (e, 0, 0))

    def next_ids(e, bg):
        nxt = jnp.minimum(e * nb + bg + 1, n_e * nb - 1)
        return (nxt % nb, nxt // nb, 0, 0)

    return pl.pallas_call(
        functools.partial(_ffn_kernel, cap=cap),
        grid=(n_e, nb),
        in_specs=[
            pl.BlockSpec((blk, 1, 1, cap), lambda e, bg: (bg, e, 0, 0), memory_space=pltpu.SMEM),
            pl.BlockSpec((blk, 1, 1, cap), next_ids, memory_space=pltpu.SMEM),
            pl.BlockSpec(memory_space=pl.ANY),
            pl.BlockSpec((blk, 1, cap, 1), lambda e, bg: (bg, e, 0, 0)),
            wspec, wspec, wspec,
        ],
        out_specs=pl.BlockSpec((blk, 1, cap, d), lambda e, bg: (bg, e, 0, 0)),
        out_shape=jax.ShapeDtypeStruct((b, n_e, cap, d), F32),
        scratch_shapes=[pltpu.VMEM((FFN_G * cap, d), F32), pltpu.VMEM((FFN_G * cap, d), F32),
                        pltpu.SemaphoreType.DMA((2,))],
        compiler_params=_params("arbitrary", "arbitrary"),
        name="ffn",
    )(idx4, idx4, xn2, gate4, wg, wu, wd)


def _combine_kernel(idx_ref, y_ref, ff_ref, *, cap):
    e = pl.program_id(1)

    @pl.when(e == 0)
    def _():
        ff_ref[...] = jnp.zeros_like(ff_ref)

    for r0 in range(0, cap, COMBINE_UNROLL):
        toks = [idx_ref[0, 0, 0, r0 + u] for u in range(COMBINE_UNROLL)]
        sums = [ff_ref[0, pl.ds(toks[u], 1), :] + y_ref[0, 0, r0 + u:r0 + u + 1, :]
                for u in range(COMBINE_UNROLL)]
        for u in range(COMBINE_UNROLL):
            ff_ref[0, pl.ds(toks[u], 1), :] = sums[u]


def _combine(idx4, y, t, cap):
    b, n_e, _, d = y.shape
    return pl.pallas_call(
        functools.partial(_combine_kernel, cap=cap),
        grid=(b, n_e),
        in_specs=[
            pl.BlockSpec((1, 1, 1, cap), lambda bi, e: (bi, e, 0, 0), memory_space=pltpu.SMEM),
            pl.BlockSpec((1, 1, cap, d), lambda bi, e: (bi, e, 0, 0)),
        ],
        out_specs=pl.BlockSpec((1, t, d), lambda bi, e: (bi, 0, 0)),
        out_shape=jax.ShapeDtypeStruct((b, t, d), F32),
        compiler_params=_params("parallel", "arbitrary"),
        name="combine",
    )(idx4, y)


def _final_kernel(h_ref, ff_ref, mod_ref, g_ref, o_ref):
    g2 = mod_ref[0, :, 5 * D_MODEL:6 * D_MODEL]
    o_ref[0] = h_ref[0] + g2 * _rms(ff_ref[0], g_ref[...])


def _final(h1, ff, mod3, g):
    b, t, d = h1.shape
    tile = pl.BlockSpec((1, TM, d), lambda bi, i: (bi, i, 0))
    return pl.pallas_call(
        _final_kernel,
        grid=(b, t // TM),
        in_specs=[tile, tile,
                  pl.BlockSpec((1, 1, 6 * d), lambda bi, i: (bi, 0, 0)),
                  pl.BlockSpec((1, d), lambda bi, i: (0, 0))],
        out_specs=tile,
        out_shape=jax.ShapeDtypeStruct((b, t, d), F32),
        compiler_params=_params("parallel", "parallel"),
        name="final",
    )(h1, ff, mod3, g)


def _rope_tables(t):
    rows = t // GRID_W
    row = jnp.repeat(jnp.arange(rows, dtype=F32), GRID_W)
    col = jnp.tile(jnp.arange(GRID_W, dtype=F32), rows)
    inv = ROPE_BASE ** (-jnp.arange(0, ROT_AXIS, 2, dtype=F32) / ROT_AXIS)
    ang_r = row[:, None] * inv[None, :]
    ang_c = col[:, None] * inv[None, :]
    ang = jnp.concatenate([ang_r, ang_r, ang_c, ang_c], axis=-1)
    ang = jnp.concatenate([ang, ang], axis=-1)
    cos, sin = jnp.cos(ang), jnp.sin(ang)
    low = (jnp.arange(LANES) % ROT_AXIS) < ROT_AXIS // 2
    sin_lo = jnp.where(low[None, :], -sin, 0.0)
    sin_hi = jnp.where(low[None, :], 0.0, sin)
    scale = jnp.float32(math.log2(math.e) / math.sqrt(HEAD_DIM))
    return jnp.concatenate([cos * scale, sin_lo * scale, sin_hi * scale, cos, sin_lo, sin_hi], axis=-1)


def kernel(x, c, ctx, c_ctx, w_ada, b_ada, norm_pre_mix, norm_post_mix, norm_pre_ffn, norm_post_ffn,
           w_in, conv_w, lambda_q1, lambda_k1, lambda_q2, lambda_k2, subln_g, w_out, w_router,
           w_gate, w_up, w_down):
    b, t, d = x.shape
    depth = w_ada.shape[0]
    assert depth == 1 and d == D_MODEL and t % TM == 0 and t % TQ_BLOCK == 0 and TQ_BLOCK % TQ == 0 and b % (2 * FFN_G) == 0
    assert b + 1 <= MOD_ROWS
    layer = 0
    lam_init = 0.8 - 0.6 * math.exp(-0.3 * layer)
    cap = CAP_FACTOR * t // N_EXPERTS

    cs = jnp.concatenate([c, c_ctx[None, :], jnp.zeros((MOD_ROWS - b - 1, d), F32)], axis=0)
    mod3 = _ada(cs, w_ada[layer], b_ada[layer][None, :]).reshape(MOD_ROWS, 1, 6 * d)

    w_in_bf = w_in[layer].astype(BF16)
    q, k, v, conv = _inproj(x, mod3, norm_pre_mix[layer][None, :], w_in_bf, conv_w[layer], _rope_tables(t))
    kc, vc = _ctxproj(ctx, mod3, norm_pre_mix[layer][None, :], w_in_bf)
    attn = _attn(q, k, v, kc, vc, lambda_q1[layer][None, :], lambda_k1[layer][None, :],
                 lambda_q2[layer][None, :], lambda_k2[layer][None, :], subln_g[layer][None, :], lam_init)
    h1, xn2, aff_t = _outproj(attn, conv, x, mod3, norm_post_mix[layer][None, :],
                              norm_pre_ffn[layer][None, :], w_out[layer].astype(BF16), w_router[layer].T)

    idx_p, gate_p = _topk(aff_t, cap)
    idx = jnp.swapaxes(idx_p[:, :, :N_EXPERTS], 1, 2)
    gate = jnp.swapaxes(gate_p[:, :, :N_EXPERTS], 1, 2)
    idx4 = idx.reshape(b, N_EXPERTS, 1, cap)
    gate4 = gate.reshape(b, N_EXPERTS, cap, 1)

    y = _ffn(idx4, xn2, gate4, w_gate[layer].astype(BF16), w_up[layer].astype(BF16),
             w_down[layer].astype(BF16), cap)
    ff = _combine(idx4, y, t, cap)
    return _final(h1, ff, mod3, norm_post_ffn[layer][None, :])
```

```python
import functools
import math

import jax
import jax.numpy as jnp
from jax import lax
from jax.experimental import pallas as pl
from jax.experimental.pallas import tpu as pltpu

F32 = jnp.float32
BF16 = jnp.bfloat16
I32 = jnp.int32

D_MODEL = 1024
CTX_LEN = 256
GRID_W = 64
HEADS = 4
HEAD_DIM = 64
V_DIM = 2 * HEAD_DIM
QK_W = HEADS * 2 * HEAD_DIM
V_W = HEADS * V_DIM
CONV_W = 512
PROJ_W = 2 * QK_W + V_W + 3 * CONV_W
ROPE_BASE = 10000.0
ROT_AXIS = HEAD_DIM // 2
N_EXPERTS = 16
CAP_FACTOR = 2
EPS = 1e-6
LANES = 128
ROW_TILES = D_MODEL // LANES
MOD_ROWS = 40

TM = 1024
TQ = 256
TQ_BLOCK = 512
FFN_G = 2
COMBINE_UNROLL = 8
VMEM_LIMIT = 48 * 1024 * 1024
HIGHEST = lax.Precision.HIGHEST
NT_DIMS = (((1,), (1,)), ((), ()))


def _params(*sem):
    return pltpu.CompilerParams(dimension_semantics=sem, vmem_limit_bytes=VMEM_LIMIT)


def _rms(x, g):
    return x * lax.rsqrt(jnp.mean(x * x, axis=-1, keepdims=True) + EPS) * g


def _silu(x):
    return x * jax.nn.sigmoid(x)


def _store_token_major(ref, lead, x):
    for j in range(ROW_TILES):
        ref[lead + (pl.ds(j, x.shape[0], stride=ROW_TILES), slice(None))] = x[:, j * LANES:(j + 1) * LANES]


def _load_token_major(ref, lead, rows):
    return jnp.concatenate([ref[lead + (pl.ds(j, rows, stride=ROW_TILES), slice(None))]
                            for j in range(ROW_TILES)], axis=1)


def _ada_kernel(cs_ref, w_ref, b_ref, o_ref):
    s = _silu(cs_ref[...])
    o_ref[...] = jnp.dot(s, w_ref[...], preferred_element_type=F32, precision=HIGHEST) + b_ref[...]


def _ada(cs, w, b):
    n = w.shape[1]
    tn = 512
    return pl.pallas_call(
        _ada_kernel,
        grid=(n // tn,),
        in_specs=[pl.BlockSpec((MOD_ROWS, D_MODEL), lambda j: (0, 0)),
                  pl.BlockSpec((D_MODEL, tn), lambda j: (0, j)),
                  pl.BlockSpec((1, tn), lambda j: (0, j))],
        out_specs=pl.BlockSpec((MOD_ROWS, tn), lambda j: (0, j)),
        out_shape=jax.ShapeDtypeStruct((MOD_ROWS, n), F32),
        compiler_params=_params("parallel"),
        name="ada",
    )(cs, w, b)


def _rope_store(p, tabs, out_ref):
    cos, sin_lo, sin_hi = tabs
    for j in range(QK_W // LANES):
        xj = p[:, j * LANES:(j + 1) * LANES]
        r = xj * cos + pltpu.roll(xj, LANES - 16, 1) * sin_lo + pltpu.roll(xj, 16, 1) * sin_hi
        out_ref[0, :, j * LANES:(j + 1) * LANES] = r.astype(BF16)


def _store_values(v_ref, pv):
    ones = jnp.ones((pv.shape[0], V_DIM), BF16)
    for h in range(HEADS):
        v_ref[0, :, 2 * h * V_DIM:(2 * h + 1) * V_DIM] = pv[:, h * V_DIM:(h + 1) * V_DIM].astype(BF16)
        v_ref[0, :, (2 * h + 1) * V_DIM:(2 * h + 2) * V_DIM] = ones


def _in_kernel(x_ref, xp_ref, xn_ref, mod_ref, g_ref, w_ref, cw_ref, tab_ref,
               q_ref, k_ref, v_ref, c_ref):
    i = pl.program_id(1)
    last = pl.num_programs(1) - 1
    g = g_ref[...]
    sh = mod_ref[0, :, 0:D_MODEL]
    sc = mod_ref[0, :, D_MODEL:2 * D_MODEL]

    def norm_mod(xt):
        return (_rms(xt, g) * (1.0 + sc) + sh).astype(BF16)

    def proj(xb, c0, width):
        return jnp.dot(xb, w_ref[:, c0:c0 + width], preferred_element_type=F32)

    xb = norm_mod(x_ref[0])
    tabs = [tab_ref[:, j * LANES:(j + 1) * LANES] for j in range(6)]
    _rope_store(proj(xb, 0, QK_W), tabs[0:3], q_ref)
    _rope_store(proj(xb, QK_W, QK_W), tabs[3:6], k_ref)
    _store_values(v_ref, proj(xb, 2 * QK_W, V_W))

    c0 = 2 * QK_W + V_W
    gb = proj(xb, c0, CONV_W)
    u = proj(xb, c0 + CONV_W, CONV_W) * proj(xb, c0 + 2 * CONV_W, CONV_W)
    xh = norm_mod(jnp.concatenate([xp_ref[0], xn_ref[0]], axis=0))
    ph = proj(xh, c0 + CONV_W, 2 * CONV_W)
    uh = ph[:, :CONV_W] * ph[:, CONV_W:]
    u_before = jnp.where(i > 0, uh[7:8], 0.0)
    u_after = jnp.where(i < last, uh[8:9], 0.0)
    rows = lax.broadcasted_iota(I32, u.shape, 0)
    u_prev = jnp.where(rows == 0, u_before, pltpu.roll(u, 1, 0))
    u_next = jnp.where(rows == TM - 1, u_after, pltpu.roll(u, TM - 1, 0))
    cw = cw_ref[...]
    y = cw[0:1] * u_prev + cw[1:2] * u + cw[2:3] * u_next
    c_ref[0] = (gb * y).astype(BF16)


def _inproj(x, mod3, g, w_bf, conv_w, tab):
    b, t, _ = x.shape
    nt = t // TM
    hb = TM // 8
    out = jax.ShapeDtypeStruct((b, t, QK_W), BF16)
    tile = pl.BlockSpec((1, TM, QK_W), lambda bi, i: (bi, i, 0))
    return pl.pallas_call(
        _in_kernel,
        grid=(b, nt),
        in_specs=[
            pl.BlockSpec((1, TM, D_MODEL), lambda bi, i: (bi, i, 0)),
            pl.BlockSpec((1, 8, D_MODEL), lambda bi, i: (bi, jnp.maximum(i * hb - 1, 0), 0)),
            pl.BlockSpec((1, 8, D_MODEL), lambda bi, i: (bi, jnp.minimum((i + 1) * hb, t // 8 - 1), 0)),
            pl.BlockSpec((1, 1, 6 * D_MODEL), lambda bi, i: (bi, 0, 0)),
            pl.BlockSpec((1, D_MODEL), lambda bi, i: (0, 0)),
            pl.BlockSpec((D_MODEL, PROJ_W), lambda bi, i: (0, 0)),
            pl.BlockSpec((3, CONV_W), lambda bi, i: (0, 0)),
            pl.BlockSpec((TM, 6 * LANES), lambda bi, i: (i, 0)),
        ],
        out_specs=[tile, tile, pl.BlockSpec((1, TM, 2 * V_W), lambda bi, i: (bi, i, 0)), tile],
        out_shape=[out, out, jax.ShapeDtypeStruct((b, t, 2 * V_W), BF16), out],
        compiler_params=_params("parallel", "parallel"),
        name="inproj",
    )(x, x, x, mod3, g, w_bf, conv_w, tab)


def _ctx_kernel(x_ref, mod_ref, g_ref, wk_ref, wv_ref, k_ref, v_ref):
    sh = mod_ref[0, :, 0:D_MODEL]
    sc = mod_ref[0, :, D_MODEL:2 * D_MODEL]
    xb = (_rms(x_ref[0], g_ref[...]) * (1.0 + sc) + sh).astype(BF16)
    k_ref[0] = jnp.dot(xb, wk_ref[...], preferred_element_type=F32).astype(BF16)
    _store_values(v_ref, jnp.dot(xb, wv_ref[...], preferred_element_type=F32))


def _ctxproj(ctx, mod3, g, w_bf):
    b = ctx.shape[0]
    out = jax.ShapeDtypeStruct((b, CTX_LEN, QK_W), BF16)
    tile = pl.BlockSpec((1, CTX_LEN, QK_W), lambda bi: (bi, 0, 0))
    return pl.pallas_call(
        _ctx_kernel,
        grid=(b,),
        in_specs=[
            pl.BlockSpec((1, CTX_LEN, D_MODEL), lambda bi: (bi, 0, 0)),
            pl.BlockSpec((1, 1, 6 * D_MODEL), lambda bi: (b, 0, 0)),
            pl.BlockSpec((1, D_MODEL), lambda bi: (0, 0)),
            pl.BlockSpec((D_MODEL, QK_W), lambda bi: (0, 1)),
            pl.BlockSpec((D_MODEL, V_W), lambda bi: (0, 2)),
        ],
        out_specs=[tile, pl.BlockSpec((1, CTX_LEN, 2 * V_W), lambda bi: (bi, 0, 0))],
        out_shape=[out, jax.ShapeDtypeStruct((b, CTX_LEN, 2 * V_W), BF16)],
        compiler_params=_params("parallel"),
        name="ctxproj",
    )(ctx, mod3, g, w_bf, w_bf)


def _attn_kernel(q_ref, k_ref, v_ref, kc_ref, vc_ref, lq1_ref, lk1_ref, lq2_ref, lk2_ref,
                 sg_ref, o_ref, *, lam_init):
    lam = (jnp.exp(jnp.sum(lq1_ref[...] * lk1_ref[...], keepdims=True))
           - jnp.exp(jnp.sum(lq2_ref[...] * lk2_ref[...], keepdims=True)) + lam_init)
    lane = lax.broadcasted_iota(I32, (1, V_DIM), 1)
    first = (lane < HEAD_DIM).astype(BF16)
    second = (lane >= HEAD_DIM).astype(BF16)
    sg = sg_ref[...] * (1.0 - lam_init)

    for h, u in [(h, u) for u in range(TQ_BLOCK // TQ) for h in range(HEADS)]:
        sl = slice(h * V_DIM, (h + 1) * V_DIM)
        sv = slice(h * 2 * V_DIM, (h + 1) * 2 * V_DIM)
        rows = slice(u * TQ, (u + 1) * TQ)
        qh = q_ref[0, rows, sl]
        q2 = jnp.concatenate([qh * first, qh * second], axis=0)
        s_l = lax.dot_general(q2, k_ref[0, :, sl], NT_DIMS, preferred_element_type=F32)
        s_c = lax.dot_general(q2, kc_ref[0, :, sl], NT_DIMS, preferred_element_type=F32)
        m = jnp.maximum(jnp.max(s_l, axis=1, keepdims=True), jnp.max(s_c, axis=1, keepdims=True))
        p_l = jnp.exp2(s_l - m).astype(BF16)
        p_c = jnp.exp2(s_c - m).astype(BF16)
        pv = (jnp.dot(p_l, v_ref[0, :, sv], preferred_element_type=F32)
              + jnp.dot(p_c, vc_ref[0, :, sv], preferred_element_type=F32))
        num = pv[:, :V_DIM]
        den = pv[:, V_DIM:V_DIM + 1]
        o = num[:TQ] * (1.0 / den[:TQ]) - num[TQ:] * (lam / den[TQ:])
        o_ref[0, rows, sl] = _rms(o, sg).astype(BF16)


def _attn(q, k, v, kc, vc, lq1, lk1, lq2, lk2, sg, lam_init):
    b, t, _ = q.shape
    qt = pl.BlockSpec((1, TQ_BLOCK, QK_W), lambda bi, i: (bi, i, 0))
    vec = pl.BlockSpec((1, HEAD_DIM), lambda bi, i: (0, 0))

    def per_batch(rows, width):
        return pl.BlockSpec((1, rows, width), lambda bi, i: (bi, 0, 0))

    return pl.pallas_call(
        functools.partial(_attn_kernel, lam_init=lam_init),
        grid=(b, t // TQ_BLOCK),
        in_specs=[qt, per_batch(t, QK_W), per_batch(t, 2 * V_W),
                  per_batch(CTX_LEN, QK_W), per_batch(CTX_LEN, 2 * V_W),
                  vec, vec, vec, vec, pl.BlockSpec((1, V_DIM), lambda bi, i: (0, 0))],
        out_specs=qt,
        out_shape=jax.ShapeDtypeStruct((b, t, V_W), BF16),
        compiler_params=_params("parallel", "parallel"),
        name="attn",
    )(q, k, v, kc, vc, lq1, lk1, lq2, lk2, sg)


def _out_kernel(a_ref, c_ref, x_ref, mod_ref, gpost_ref, gpre_ref, wo_ref, wr_ref,
                h_ref, xn_ref, aff_ref):
    mix = (jnp.dot(a_ref[0], wo_ref[0:V_W, :], preferred_element_type=F32)
           + jnp.dot(c_ref[0], wo_ref[V_W:V_W + CONV_W, :], preferred_element_type=F32))
    g1 = mod_ref[0, :, 2 * D_MODEL:3 * D_MODEL]
    sh2 = mod_ref[0, :, 3 * D_MODEL:4 * D_MODEL]
    sc2 = mod_ref[0, :, 4 * D_MODEL:5 * D_MODEL]
    h1 = x_ref[0] + g1 * _rms(mix, gpost_ref[...])
    h_ref[0] = h1
    xn2 = _rms(h1, gpre_ref[...]) * (1.0 + sc2) + sh2
    _store_token_major(xn_ref, (0,), xn2)
    def split(a):
        hi = a.astype(BF16)
        return hi, (a - hi.astype(F32)).astype(BF16)

    w_hi, w_lo = split(wr_ref[...])
    x_hi, x_lo = split(xn2)
    both = lax.dot_general(jnp.concatenate([w_hi, w_lo], axis=0), x_hi, NT_DIMS,
                           preferred_element_type=F32)
    logits = (both[:N_EXPERTS] + both[N_EXPERTS:]
              + lax.dot_general(w_hi, x_lo, NT_DIMS, preferred_element_type=F32))
    e = jnp.exp(logits - jnp.max(logits, axis=0, keepdims=True))
    aff_ref[0] = e / jnp.sum(e, axis=0, keepdims=True)


def _outproj(attn, conv, x, mod3, gpost, gpre, wo_bf, wr_t):
    b, t, _ = x.shape
    half = pl.BlockSpec((1, TM, V_W), lambda bi, i: (bi, i, 0))
    tile = pl.BlockSpec((1, TM, D_MODEL), lambda bi, i: (bi, i, 0))
    row = pl.BlockSpec((1, D_MODEL), lambda bi, i: (0, 0))
    act = jax.ShapeDtypeStruct((b, t, D_MODEL), F32)
    return pl.pallas_call(
        _out_kernel,
        grid=(b, t // TM),
        in_specs=[half, half, tile,
                  pl.BlockSpec((1, 1, 6 * D_MODEL), lambda bi, i: (bi, 0, 0)),
                  row, row,
                  pl.BlockSpec((V_W + CONV_W, D_MODEL), lambda bi, i: (0, 0)),
                  pl.BlockSpec((N_EXPERTS, D_MODEL), lambda bi, i: (0, 0))],
        out_specs=[tile, pl.BlockSpec((1, TM * ROW_TILES, LANES), lambda bi, i: (bi, i, 0)),
                   pl.BlockSpec((1, N_EXPERTS, TM), lambda bi, i: (bi, 0, i))],
        out_shape=[act, jax.ShapeDtypeStruct((b, t * ROW_TILES, LANES), F32),
                   jax.ShapeDtypeStruct((b, N_EXPERTS, t), F32)],
        compiler_params=_params("parallel", "parallel"),
        name="outproj",
    )(attn, conv, x, mod3, gpost, gpre, wo_bf, wr_t)


def _prefix_exclusive(mask_f):
    r = lax.broadcasted_iota(I32, (LANES, LANES), 0)
    c = lax.broadcasted_iota(I32, (LANES, LANES), 1)
    upper = (r <= c).astype(BF16)
    off = jnp.zeros((mask_f.shape[0], 1), F32)
    outs = []
    for j in range(mask_f.shape[1] // LANES):
        blk = mask_f[:, j * LANES:(j + 1) * LANES]
        inc = jnp.dot(blk.astype(BF16), upper, preferred_element_type=F32)
        outs.append(inc - blk + off)
        off = off + inc[:, LANES - 1:LANES]
    return jnp.concatenate(outs, axis=1)


def _topk_kernel(aff_ref, idx_ref, gate_ref, *, cap):
    aff = aff_ref[0]
    n_e, t = aff.shape

    def search(i, thr_bits):
        cand = thr_bits | (jnp.int32(1) << (29 - i))
        cnt = jnp.sum((aff >= pltpu.bitcast(cand, F32)).astype(F32), axis=1, keepdims=True)
        return jnp.where(cnt >= cap, cand, thr_bits)

    thr = pltpu.bitcast(lax.fori_loop(0, 30, search, jnp.zeros((n_e, 1), I32)), F32)
    gt = aff > thr
    eq = aff == thr
    need = cap - jnp.sum(gt.astype(F32), axis=1, keepdims=True)
    eq_f = eq.astype(F32)
    take_eq = jnp.where(_prefix_exclusive(eq_f) < need, eq_f, 0.0)
    sel_f = jnp.where(gt, 1.0, take_eq)
    slot = jnp.where(sel_f > 0.0, _prefix_exclusive(sel_f), -1.0)

    tok = lax.broadcasted_iota(I32, (cap, t), 1).astype(F32)
    slot_ids = lax.broadcasted_iota(I32, (cap, 1), 0).astype(F32)
    lane = lax.broadcasted_iota(I32, (cap, LANES), 1)
    idx_acc = jnp.zeros((cap, LANES), F32)
    gate_acc = jnp.zeros((cap, LANES), F32)
    for e in range(n_e):
        hit = slot[e:e + 1, :] == slot_ids
        idx_col = jnp.sum(jnp.where(hit, tok, 0.0), axis=1, keepdims=True)
        gate_col = jnp.sum(jnp.where(hit, aff[e:e + 1, :], 0.0), axis=1, keepdims=True)
        idx_acc = jnp.where(lane == e, idx_col, idx_acc)
        gate_acc = jnp.where(lane == e, gate_col, gate_acc)
    idx_ref[0] = idx_acc.astype(I32)
    gate_ref[0] = gate_acc


def _topk(aff_t, cap):
    b, n_e, t = aff_t.shape
    out = pl.BlockSpec((1, cap, LANES), lambda bi: (bi, 0, 0))
    return pl.pallas_call(
        functools.partial(_topk_kernel, cap=cap),
        grid=(b,),
        in_specs=[pl.BlockSpec((1, n_e, t), lambda bi: (bi, 0, 0))],
        out_specs=[out, out],
        out_shape=[jax.ShapeDtypeStruct((b, cap, LANES), I32),
                   jax.ShapeDtypeStruct((b, cap, LANES), F32)],
        compiler_params=_params("parallel"),
        name="topk",
    )(aff_t)


def _ffn_kernel(idx_ref, idx_next_ref, xn_hbm, gate_ref, wg_ref, wu_ref, wd_ref, y_ref,
                xs_a, xs_b, sems, *, cap):
    nb = pl.num_programs(1)
    step = pl.program_id(0) * nb + pl.program_id(1)
    last = pl.num_programs(0) * nb - 1
    rows = FFN_G * cap
    bufs = (xs_a, xs_b)

    def row_copy(ids_ref, batch0, g, r, half):
        tkn = ids_ref[half * FFN_G + g, 0, 0, r]
        src_row = pl.multiple_of(tkn * ROW_TILES, ROW_TILES)
        return pltpu.make_async_copy(xn_hbm.at[batch0 + half * FFN_G + g, pl.ds(src_row, ROW_TILES), :],
                                     bufs[half].at[pl.ds((g * cap + r) * ROW_TILES, ROW_TILES), :],
                                     sems.at[half])

    def wait_half(half):
        pltpu.make_async_copy(xn_hbm.at[0, pl.ds(0, rows * ROW_TILES), :], bufs[half], sems.at[half]).wait()

    def expert_up(half):
        xs = _load_token_major(bufs[half], (), rows).astype(BF16)
        hg = jnp.dot(xs, wg_ref[0], preferred_element_type=F32)
        hu = jnp.dot(xs, wu_ref[0], preferred_element_type=F32)
        return (_silu(hg) * hu).astype(BF16)

    def expert_down(half, h):
        y = jnp.dot(h, wd_ref[0], preferred_element_type=F32)
        for g in range(FFN_G):
            k = half * FFN_G + g
            _store_token_major(y_ref, (k, 0), y[g * cap:(g + 1) * cap] * gate_ref[k, 0])

    batch0 = pl.program_id(1) * (2 * FFN_G)

    @pl.when(step == 0)
    def _():
        for g in range(FFN_G):
            def issue(r, carry, g=g):
                row_copy(idx_ref, batch0, g, r, 0).start()
                return carry
            lax.fori_loop(0, cap, issue, 0)
        wait_half(0)

    for g in range(FFN_G):
        for r in range(cap):
            row_copy(idx_ref, batch0, g, r, 1).start()
    h0 = expert_up(0)
    expert_down(0, h0)
    wait_half(1)

    nxt_batch0 = (jnp.minimum(step + 1, last) % nb) * (2 * FFN_G)
    for g in range(FFN_G):
        for r in range(cap):
            row_copy(idx_next_ref, nxt_batch0, g, r, 0).start()
    h1 = expert_up(1)
    wait_half(0)
    expert_down(1, h1)


def _ffn(idx4, xn2, gate4, wg, wu, wd, cap):
    b = xn2.shape[0]
    n_e, d, _ = wg.shape
    blk = 2 * FFN_G
    nb = b // blk
    wspec = pl.BlockSpec((1, d, d), lambda e, bg: (e, 0, 0))

    def next_ids(e, bg):
        nxt = jnp.minimum(e * nb + bg + 1, n_e * nb - 1)
        return (nxt % nb, nxt // nb, 0, 0)

    return pl.pallas_call(
        functools.partial(_ffn_kernel, cap=cap),
        grid=(n_e, nb),
        in_specs=[
            pl.BlockSpec((blk, 1, 1, cap), lambda e, bg: (bg, e, 0, 0), memory_space=pltpu.SMEM),
            pl.BlockSpec((blk, 1, 1, cap), next_ids, memory_space=pltpu.SMEM),
            pl.BlockSpec(memory_space=pl.ANY),
            pl.BlockSpec((blk, 1, cap, 1), lambda e, bg: (bg, e, 0, 0)),
            wspec, wspec, wspec,
        ],
        out_specs=pl.BlockSpec((blk, 1, cap * ROW_TILES, LANES), lambda e, bg: (bg, e, 0, 0)),
        out_shape=jax.ShapeDtypeStruct((b, n_e, cap * ROW_TILES, LANES), F32),
        scratch_shapes=[pltpu.VMEM((FFN_G * cap * ROW_TILES, LANES), F32),
                        pltpu.VMEM((FFN_G * cap * ROW_TILES, LANES), F32),
                        pltpu.SemaphoreType.DMA((2,))],
        compiler_params=_params("arbitrary", "arbitrary"),
        name="ffn",
    )(idx4, idx4, xn2, gate4, wg, wu, wd)


def _combine_kernel(idx_ref, y_ref, ff_ref, *, cap):
    e = pl.program_id(1)

    @pl.when(e == 0)
    def _():
        ff_ref[...] = jnp.zeros_like(ff_ref)

    for r0 in range(0, cap, COMBINE_UNROLL):
        dst = [pl.ds(pl.multiple_of(idx_ref[0, 0, 0, r0 + u] * ROW_TILES, ROW_TILES), ROW_TILES)
               for u in range(COMBINE_UNROLL)]
        sums = [ff_ref[0, dst[u], :] + y_ref[0, 0, (r0 + u) * ROW_TILES:(r0 + u + 1) * ROW_TILES, :]
                for u in range(COMBINE_UNROLL)]
        for u in range(COMBINE_UNROLL):
            ff_ref[0, dst[u], :] = sums[u]


def _combine(idx4, y, t, cap):
    b, n_e = y.shape[:2]
    return pl.pallas_call(
        functools.partial(_combine_kernel, cap=cap),
        grid=(b, n_e),
        in_specs=[
            pl.BlockSpec((1, 1, 1, cap), lambda bi, e: (bi, e, 0, 0), memory_space=pltpu.SMEM),
            pl.BlockSpec((1, 1, cap * ROW_TILES, LANES), lambda bi, e: (bi, e, 0, 0)),
        ],
        out_specs=pl.BlockSpec((1, t * ROW_TILES, LANES), lambda bi, e: (bi, 0, 0)),
        out_shape=jax.ShapeDtypeStruct((b, t * ROW_TILES, LANES), F32),
        compiler_params=_params("parallel", "arbitrary"),
        name="combine",
    )(idx4, y)


def _final_kernel(h_ref, ff_ref, mod_ref, g_ref, o_ref):
    g2 = mod_ref[0, :, 5 * D_MODEL:6 * D_MODEL]
    ff = _load_token_major(ff_ref, (0,), h_ref.shape[1])
    o_ref[0] = h_ref[0] + g2 * _rms(ff, g_ref[...])


def _final(h1, ff, mod3, g):
    b, t, d = h1.shape
    tile = pl.BlockSpec((1, TM, d), lambda bi, i: (bi, i, 0))
    return pl.pallas_call(
        _final_kernel,
        grid=(b, t // TM),
        in_specs=[tile, pl.BlockSpec((1, TM * ROW_TILES, LANES), lambda bi, i: (bi, i, 0)),
                  pl.BlockSpec((1, 1, 6 * d), lambda bi, i: (bi, 0, 0)),
                  pl.BlockSpec((1, d), lambda bi, i: (0, 0))],
        out_specs=tile,
        out_shape=jax.ShapeDtypeStruct((b, t, d), F32),
        compiler_params=_params("parallel", "parallel"),
        name="final",
    )(h1, ff, mod3, g)


def _rope_tables(t):
    rows = t // GRID_W
    row = jnp.repeat(jnp.arange(rows, dtype=F32), GRID_W)
    col = jnp.tile(jnp.arange(GRID_W, dtype=F32), rows)
    inv = ROPE_BASE ** (-jnp.arange(0, ROT_AXIS, 2, dtype=F32) / ROT_AXIS)
    ang_r = row[:, None] * inv[None, :]
    ang_c = col[:, None] * inv[None, :]
    ang = jnp.concatenate([ang_r, ang_r, ang_c, ang_c], axis=-1)
    ang = jnp.concatenate([ang, ang], axis=-1)
    cos, sin = jnp.cos(ang), jnp.sin(ang)
    low = (jnp.arange(LANES) % ROT_AXIS) < ROT_AXIS // 2
    sin_lo = jnp.where(low[None, :], -sin, 0.0)
    sin_hi = jnp.where(low[None, :], 0.0, sin)
    scale = jnp.float32(math.log2(math.e) / math.sqrt(HEAD_DIM))
    return jnp.concatenate([cos * scale, sin_lo * scale, sin_hi * scale, cos, sin_lo, sin_hi], axis=-1)


def kernel(x, c, ctx, c_ctx, w_ada, b_ada, norm_pre_mix, norm_post_mix, norm_pre_ffn, norm_post_ffn,
           w_in, conv_w, lambda_q1, lambda_k1, lambda_q2, lambda_k2, subln_g, w_out, w_router,
           w_gate, w_up, w_down):
    b, t, d = x.shape
    depth = w_ada.shape[0]
    assert depth == 1 and d == D_MODEL and t % TM == 0 and t % TQ_BLOCK == 0 and TQ_BLOCK % TQ == 0 and b % (2 * FFN_G) == 0
    assert b + 1 <= MOD_ROWS
    layer = 0
    lam_init = 0.8 - 0.6 * math.exp(-0.3 * layer)
    cap = CAP_FACTOR * t // N_EXPERTS

    cs = jnp.concatenate([c, c_ctx[None, :], jnp.zeros((MOD_ROWS - b - 1, d), F32)], axis=0)
    mod3 = _ada(cs, w_ada[layer], b_ada[layer][None, :]).reshape(MOD_ROWS, 1, 6 * d)

    w_in_bf = w_in[layer].astype(BF16)
    q, k, v, conv = _inproj(x, mod3, norm_pre_mix[layer][None, :], w_in_bf, conv_w[layer], _rope_tables(t))
    kc, vc = _ctxproj(ctx, mod3, norm_pre_mix[layer][None, :], w_in_bf)
    attn = _attn(q, k, v, kc, vc, lambda_q1[layer][None, :], lambda_k1[layer][None, :],
                 lambda_q2[layer][None, :], lambda_k2[layer][None, :], subln_g[layer][None, :], lam_init)
    h1, xn2, aff_t = _outproj(attn, conv, x, mod3, norm_post_mix[layer][None, :],
                              norm_pre_ffn[layer][None, :], w_out[layer].astype(BF16), w_router[layer].T)

    idx_p, gate_p = _topk(aff_t, cap)
    idx = jnp.swapaxes(idx_p[:, :, :N_EXPERTS], 1, 2)
    gate = jnp.swapaxes(gate_p[:, :, :N_EXPERTS], 1, 2)
    idx4 = idx.reshape(b, N_EXPERTS, 1, cap)
    gate4 = gate.reshape(b, N_EXPERTS, cap, 1)

    y = _ffn(idx4, xn2, gate4, w_gate[layer].astype(BF16), w_up[layer].astype(BF16),
             w_down[layer].astype(BF16), cap)
    ff = _combine(idx4, y, t, cap)
    return _final(h1, ff, mod3, norm_post_ffn[layer][None, :])
```

```python
import functools
import math

import jax
import jax.numpy as jnp
from jax import lax
from jax.experimental import pallas as pl
from jax.experimental.pallas import tpu as pltpu

F32 = jnp.float32
BF16 = jnp.bfloat16
I32 = jnp.int32

D_MODEL = 1024
CTX_LEN = 256
GRID_W = 64
HEADS = 4
HEAD_DIM = 64
V_DIM = 2 * HEAD_DIM
QK_W = HEADS * 2 * HEAD_DIM
V_W = HEADS * V_DIM
CONV_W = 512
PROJ_W = 2 * QK_W + V_W + 3 * CONV_W
ROPE_BASE = 10000.0
ROT_AXIS = HEAD_DIM // 2
N_EXPERTS = 16
CAP_FACTOR = 2
EPS = 1e-6
LANES = 128
ROW_TILES = D_MODEL // LANES
MOD_ROWS = 40

TM = 1024
TQ = 256
TQ_BLOCK = 512
FFN_G = 2
COMBINE_UNROLL = 8
COMBINE_EXPERTS = 4
VMEM_LIMIT = 48 * 1024 * 1024
HIGHEST = lax.Precision.HIGHEST
NT_DIMS = (((1,), (1,)), ((), ()))


def _params(*sem):
    return pltpu.CompilerParams(dimension_semantics=sem, vmem_limit_bytes=VMEM_LIMIT)


def _rms(x, g):
    return x * lax.rsqrt(jnp.mean(x * x, axis=-1, keepdims=True) + EPS) * g


def _silu(x):
    return x * jax.nn.sigmoid(x)


def _store_token_major(ref, lead, x):
    for j in range(ROW_TILES):
        ref[lead + (pl.ds(j, x.shape[0], stride=ROW_TILES), slice(None))] = x[:, j * LANES:(j + 1) * LANES]


def _load_token_major(ref, lead, rows):
    return jnp.concatenate([ref[lead + (pl.ds(j, rows, stride=ROW_TILES), slice(None))]
                            for j in range(ROW_TILES)], axis=1)


def _ada_kernel(cs_ref, w_ref, b_ref, o_ref):
    s = _silu(cs_ref[...])
    o_ref[...] = jnp.dot(s, w_ref[...], preferred_element_type=F32, precision=HIGHEST) + b_ref[...]


def _ada(cs, w, b):
    n = w.shape[1]
    tn = 512
    return pl.pallas_call(
        _ada_kernel,
        grid=(n // tn,),
        in_specs=[pl.BlockSpec((MOD_ROWS, D_MODEL), lambda j: (0, 0)),
                  pl.BlockSpec((D_MODEL, tn), lambda j: (0, j)),
                  pl.BlockSpec((1, tn), lambda j: (0, j))],
        out_specs=pl.BlockSpec((MOD_ROWS, tn), lambda j: (0, j)),
        out_shape=jax.ShapeDtypeStruct((MOD_ROWS, n), F32),
        compiler_params=_params("parallel"),
        name="ada",
    )(cs, w, b)


def _rope_store(p, tabs, out_ref):
    cos, sin_lo, sin_hi = tabs
    for j in range(QK_W // LANES):
        xj = p[:, j * LANES:(j + 1) * LANES]
        r = xj * cos + pltpu.roll(xj, LANES - 16, 1) * sin_lo + pltpu.roll(xj, 16, 1) * sin_hi
        out_ref[0, :, j * LANES:(j + 1) * LANES] = r.astype(BF16)


def _store_values(v_ref, pv):
    ones = jnp.ones((pv.shape[0], V_DIM), BF16)
    for h in range(HEADS):
        v_ref[0, :, 2 * h * V_DIM:(2 * h + 1) * V_DIM] = pv[:, h * V_DIM:(h + 1) * V_DIM].astype(BF16)
        v_ref[0, :, (2 * h + 1) * V_DIM:(2 * h + 2) * V_DIM] = ones


def _in_kernel(x_ref, xp_ref, xn_ref, mod_ref, g_ref, w_ref, cw_ref, tab_ref,
               q_ref, k_ref, v_ref, c_ref):
    i = pl.program_id(1)
    last = pl.num_programs(1) - 1
    g = g_ref[...]
    sh = mod_ref[0, :, 0:D_MODEL]
    sc = mod_ref[0, :, D_MODEL:2 * D_MODEL]

    def norm_mod(xt):
        return (_rms(xt, g) * (1.0 + sc) + sh).astype(BF16)

    def proj(xb, c0, width):
        return jnp.dot(xb, w_ref[:, c0:c0 + width], preferred_element_type=F32)

    xb = norm_mod(x_ref[0])
    tabs = [tab_ref[:, j * LANES:(j + 1) * LANES] for j in range(6)]
    _rope_store(proj(xb, 0, QK_W), tabs[0:3], q_ref)
    _rope_store(proj(xb, QK_W, QK_W), tabs[3:6], k_ref)
    _store_values(v_ref, proj(xb, 2 * QK_W, V_W))

    c0 = 2 * QK_W + V_W
    gb = proj(xb, c0, CONV_W)
    u = proj(xb, c0 + CONV_W, CONV_W) * proj(xb, c0 + 2 * CONV_W, CONV_W)
    xh = norm_mod(jnp.concatenate([xp_ref[0], xn_ref[0]], axis=0))
    ph = proj(xh, c0 + CONV_W, 2 * CONV_W)
    uh = ph[:, :CONV_W] * ph[:, CONV_W:]
    u_before = jnp.where(i > 0, uh[7:8], 0.0)
    u_after = jnp.where(i < last, uh[8:9], 0.0)
    rows = lax.broadcasted_iota(I32, u.shape, 0)
    u_prev = jnp.where(rows == 0, u_before, pltpu.roll(u, 1, 0))
    u_next = jnp.where(rows == TM - 1, u_after, pltpu.roll(u, TM - 1, 0))
    cw = cw_ref[...]
    y = cw[0:1] * u_prev + cw[1:2] * u + cw[2:3] * u_next
    c_ref[0] = (gb * y).astype(BF16)


def _inproj(x, mod3, g, w_bf, conv_w, tab):
    b, t, _ = x.shape
    nt = t // TM
    hb = TM // 8
    out = jax.ShapeDtypeStruct((b, t, QK_W), BF16)
    tile = pl.BlockSpec((1, TM, QK_W), lambda bi, i: (bi, i, 0))
    return pl.pallas_call(
        _in_kernel,
        grid=(b, nt),
        in_specs=[
            pl.BlockSpec((1, TM, D_MODEL), lambda bi, i: (bi, i, 0)),
            pl.BlockSpec((1, 8, D_MODEL), lambda bi, i: (bi, jnp.maximum(i * hb - 1, 0), 0)),
            pl.BlockSpec((1, 8, D_MODEL), lambda bi, i: (bi, jnp.minimum((i + 1) * hb, t // 8 - 1), 0)),
            pl.BlockSpec((1, 1, 6 * D_MODEL), lambda bi, i: (bi, 0, 0)),
            pl.BlockSpec((1, D_MODEL), lambda bi, i: (0, 0)),
            pl.BlockSpec((D_MODEL, PROJ_W), lambda bi, i: (0, 0)),
            pl.BlockSpec((3, CONV_W), lambda bi, i: (0, 0)),
            pl.BlockSpec((TM, 6 * LANES), lambda bi, i: (i, 0)),
        ],
        out_specs=[tile, tile, pl.BlockSpec((1, TM, 2 * V_W), lambda bi, i: (bi, i, 0)), tile],
        out_shape=[out, out, jax.ShapeDtypeStruct((b, t, 2 * V_W), BF16), out],
        compiler_params=_params("parallel", "parallel"),
        name="inproj",
    )(x, x, x, mod3, g, w_bf, conv_w, tab)


def _ctx_kernel(x_ref, mod_ref, g_ref, wk_ref, wv_ref, k_ref, v_ref):
    sh = mod_ref[0, :, 0:D_MODEL]
    sc = mod_ref[0, :, D_MODEL:2 * D_MODEL]
    xb = (_rms(x_ref[0], g_ref[...]) * (1.0 + sc) + sh).astype(BF16)
    k_ref[0] = jnp.dot(xb, wk_ref[...], preferred_element_type=F32).astype(BF16)
    _store_values(v_ref, jnp.dot(xb, wv_ref[...], preferred_element_type=F32))


def _ctxproj(ctx, mod3, g, w_bf):
    b = ctx.shape[0]
    out = jax.ShapeDtypeStruct((b, CTX_LEN, QK_W), BF16)
    tile = pl.BlockSpec((1, CTX_LEN, QK_W), lambda bi: (bi, 0, 0))
    return pl.pallas_call(
        _ctx_kernel,
        grid=(b,),
        in_specs=[
            pl.BlockSpec((1, CTX_LEN, D_MODEL), lambda bi: (bi, 0, 0)),
            pl.BlockSpec((1, 1, 6 * D_MODEL), lambda bi: (b, 0, 0)),
            pl.BlockSpec((1, D_MODEL), lambda bi: (0, 0)),
            pl.BlockSpec((D_MODEL, QK_W), lambda bi: (0, 1)),
            pl.BlockSpec((D_MODEL, V_W), lambda bi: (0, 2)),
        ],
        out_specs=[tile, pl.BlockSpec((1, CTX_LEN, 2 * V_W), lambda bi: (bi, 0, 0))],
        out_shape=[out, jax.ShapeDtypeStruct((b, CTX_LEN, 2 * V_W), BF16)],
        compiler_params=_params("parallel"),
        name="ctxproj",
    )(ctx, mod3, g, w_bf, w_bf)


def _attn_kernel(q_ref, k_ref, v_ref, kc_ref, vc_ref, lq1_ref, lk1_ref, lq2_ref, lk2_ref,
                 sg_ref, o_ref, *, lam_init):
    lam = (jnp.exp(jnp.sum(lq1_ref[...] * lk1_ref[...], keepdims=True))
           - jnp.exp(jnp.sum(lq2_ref[...] * lk2_ref[...], keepdims=True)) + lam_init)
    lane = lax.broadcasted_iota(I32, (1, V_DIM), 1)
    first = (lane < HEAD_DIM).astype(BF16)
    second = (lane >= HEAD_DIM).astype(BF16)
    sg = sg_ref[...] * (1.0 - lam_init)

    for h, u in [(h, u) for u in range(TQ_BLOCK // TQ) for h in range(HEADS)]:
        sl = slice(h * V_DIM, (h + 1) * V_DIM)
        sv = slice(h * 2 * V_DIM, (h + 1) * 2 * V_DIM)
        rows = slice(u * TQ, (u + 1) * TQ)
        qh = q_ref[0, rows, sl]
        q2 = jnp.concatenate([qh * first, qh * second], axis=0)
        s_l = lax.dot_general(q2, k_ref[0, :, sl], NT_DIMS, preferred_element_type=F32)
        s_c = lax.dot_general(q2, kc_ref[0, :, sl], NT_DIMS, preferred_element_type=F32)
        m = jnp.maximum(jnp.max(s_l, axis=1, keepdims=True), jnp.max(s_c, axis=1, keepdims=True))
        p_l = jnp.exp2(s_l - m).astype(BF16)
        p_c = jnp.exp2(s_c - m).astype(BF16)
        pv = (jnp.dot(p_l, v_ref[0, :, sv], preferred_element_type=F32)
              + jnp.dot(p_c, vc_ref[0, :, sv], preferred_element_type=F32))
        num = pv[:, :V_DIM]
        den = pv[:, V_DIM:V_DIM + 1]
        o = num[:TQ] * (1.0 / den[:TQ]) - num[TQ:] * (lam / den[TQ:])
        o_ref[0, rows, sl] = _rms(o, sg).astype(BF16)


def _attn(q, k, v, kc, vc, lq1, lk1, lq2, lk2, sg, lam_init):
    b, t, _ = q.shape
    qt = pl.BlockSpec((1, TQ_BLOCK, QK_W), lambda bi, i: (bi, i, 0))
    vec = pl.BlockSpec((1, HEAD_DIM), lambda bi, i: (0, 0))

    def per_batch(rows, width):
        return pl.BlockSpec((1, rows, width), lambda bi, i: (bi, 0, 0))

    return pl.pallas_call(
        functools.partial(_attn_kernel, lam_init=lam_init),
        grid=(b, t // TQ_BLOCK),
        in_specs=[qt, per_batch(t, QK_W), per_batch(t, 2 * V_W),
                  per_batch(CTX_LEN, QK_W), per_batch(CTX_LEN, 2 * V_W),
                  vec, vec, vec, vec, pl.BlockSpec((1, V_DIM), lambda bi, i: (0, 0))],
        out_specs=qt,
        out_shape=jax.ShapeDtypeStruct((b, t, V_W), BF16),
        compiler_params=_params("parallel", "parallel"),
        name="attn",
    )(q, k, v, kc, vc, lq1, lk1, lq2, lk2, sg)


def _out_kernel(a_ref, c_ref, x_ref, mod_ref, gpost_ref, gpre_ref, wo_ref, wr_ref,
                h_ref, xn_ref, aff_ref):
    mix = (jnp.dot(a_ref[0], wo_ref[0:V_W, :], preferred_element_type=F32)
           + jnp.dot(c_ref[0], wo_ref[V_W:V_W + CONV_W, :], preferred_element_type=F32))
    g1 = mod_ref[0, :, 2 * D_MODEL:3 * D_MODEL]
    sh2 = mod_ref[0, :, 3 * D_MODEL:4 * D_MODEL]
    sc2 = mod_ref[0, :, 4 * D_MODEL:5 * D_MODEL]
    h1 = x_ref[0] + g1 * _rms(mix, gpost_ref[...])
    h_ref[0] = h1
    xn2 = _rms(h1, gpre_ref[...]) * (1.0 + sc2) + sh2
    _store_token_major(xn_ref, (0,), xn2)
    def split(a):
        hi = a.astype(BF16)
        return hi, (a - hi.astype(F32)).astype(BF16)

    w_hi, w_lo = split(wr_ref[...])
    x_hi, x_lo = split(xn2)
    both = lax.dot_general(jnp.concatenate([w_hi, w_lo], axis=0), x_hi, NT_DIMS,
                           preferred_element_type=F32)
    logits = (both[:N_EXPERTS] + both[N_EXPERTS:]
              + lax.dot_general(w_hi, x_lo, NT_DIMS, preferred_element_type=F32))
    e = jnp.exp(logits - jnp.max(logits, axis=0, keepdims=True))
    aff_ref[0] = e / jnp.sum(e, axis=0, keepdims=True)


def _outproj(attn, conv, x, mod3, gpost, gpre, wo_bf, wr_t):
    b, t, _ = x.shape
    half = pl.BlockSpec((1, TM, V_W), lambda bi, i: (bi, i, 0))
    tile = pl.BlockSpec((1, TM, D_MODEL), lambda bi, i: (bi, i, 0))
    row = pl.BlockSpec((1, D_MODEL), lambda bi, i: (0, 0))
    act = jax.ShapeDtypeStruct((b, t, D_MODEL), F32)
    return pl.pallas_call(
        _out_kernel,
        grid=(b, t // TM),
        in_specs=[half, half, tile,
                  pl.BlockSpec((1, 1, 6 * D_MODEL), lambda bi, i: (bi, 0, 0)),
                  row, row,
                  pl.BlockSpec((V_W + CONV_W, D_MODEL), lambda bi, i: (0, 0)),
                  pl.BlockSpec((N_EXPERTS, D_MODEL), lambda bi, i: (0, 0))],
        out_specs=[tile, pl.BlockSpec((1, TM * ROW_TILES, LANES), lambda bi, i: (bi, i, 0)),
                   pl.BlockSpec((1, N_EXPERTS, TM), lambda bi, i: (bi, 0, i))],
        out_shape=[act, jax.ShapeDtypeStruct((b, t * ROW_TILES, LANES), F32),
                   jax.ShapeDtypeStruct((b, N_EXPERTS, t), F32)],
        compiler_params=_params("parallel", "parallel"),
        name="outproj",
    )(attn, conv, x, mod3, gpost, gpre, wo_bf, wr_t)


def _prefix_exclusive(mask_f):
    r = lax.broadcasted_iota(I32, (LANES, LANES), 0)
    c = lax.broadcasted_iota(I32, (LANES, LANES), 1)
    upper = (r <= c).astype(BF16)
    off = jnp.zeros((mask_f.shape[0], 1), F32)
    outs = []
    for j in range(mask_f.shape[1] // LANES):
        blk = mask_f[:, j * LANES:(j + 1) * LANES]
        inc = jnp.dot(blk.astype(BF16), upper, preferred_element_type=F32)
        outs.append(inc - blk + off)
        off = off + inc[:, LANES - 1:LANES]
    return jnp.concatenate(outs, axis=1)


def _topk_kernel(aff_ref, idx_ref, gate_ref, *, cap):
    aff = aff_ref[0]
    n_e, t = aff.shape

    def search(i, thr_bits):
        cand = thr_bits | (jnp.int32(1) << (29 - i))
        cnt = jnp.sum((aff >= pltpu.bitcast(cand, F32)).astype(F32), axis=1, keepdims=True)
        return jnp.where(cnt >= cap, cand, thr_bits)

    thr = pltpu.bitcast(lax.fori_loop(0, 30, search, jnp.zeros((n_e, 1), I32)), F32)
    gt = aff > thr
    eq = aff == thr
    need = cap - jnp.sum(gt.astype(F32), axis=1, keepdims=True)
    eq_f = eq.astype(F32)
    take_eq = jnp.where(_prefix_exclusive(eq_f) < need, eq_f, 0.0)
    sel_f = jnp.where(gt, 1.0, take_eq)
    slot = jnp.where(sel_f > 0.0, _prefix_exclusive(sel_f), -1.0)

    tok = lax.broadcasted_iota(I32, (cap, t), 1).astype(F32)
    slot_ids = lax.broadcasted_iota(I32, (cap, 1), 0).astype(F32)
    lane = lax.broadcasted_iota(I32, (cap, LANES), 1)
    idx_acc = jnp.zeros((cap, LANES), F32)
    gate_acc = jnp.zeros((cap, LANES), F32)
    for e in range(n_e):
        hit = slot[e:e + 1, :] == slot_ids
        idx_col = jnp.sum(jnp.where(hit, tok, 0.0), axis=1, keepdims=True)
        gate_col = jnp.sum(jnp.where(hit, aff[e:e + 1, :], 0.0), axis=1, keepdims=True)
        idx_acc = jnp.where(lane == e, idx_col, idx_acc)
        gate_acc = jnp.where(lane == e, gate_col, gate_acc)
    idx_ref[0] = idx_acc.astype(I32)
    gate_ref[0] = gate_acc


def _topk(aff_t, cap):
    b, n_e, t = aff_t.shape
    out = pl.BlockSpec((1, cap, LANES), lambda bi: (bi, 0, 0))
    return pl.pallas_call(
        functools.partial(_topk_kernel, cap=cap),
        grid=(b,),
        in_specs=[pl.BlockSpec((1, n_e, t), lambda bi: (bi, 0, 0))],
        out_specs=[out, out],
        out_shape=[jax.ShapeDtypeStruct((b, cap, LANES), I32),
                   jax.ShapeDtypeStruct((b, cap, LANES), F32)],
        compiler_params=_params("parallel"),
        name="topk",
    )(aff_t)


def _ffn_kernel(idx_ref, idx_next_ref, xn_hbm, wg_ref, wu_ref, wd_ref, y_ref,
                xs_a, xs_b, sems, *, cap):
    nb = pl.num_programs(1)
    step = pl.program_id(0) * nb + pl.program_id(1)
    last = pl.num_programs(0) * nb - 1
    rows = FFN_G * cap
    bufs = (xs_a, xs_b)

    def row_copy(ids_ref, batch0, g, r, half):
        src_row = pl.multiple_of(ids_ref[half * FFN_G + g, 0, 0, r], ROW_TILES)
        return pltpu.make_async_copy(xn_hbm.at[batch0 + half * FFN_G + g, pl.ds(src_row, ROW_TILES), :],
                                     bufs[half].at[pl.ds((g * cap + r) * ROW_TILES, ROW_TILES), :],
                                     sems.at[half])

    def wait_half(half):
        pltpu.make_async_copy(xn_hbm.at[0, pl.ds(0, rows * ROW_TILES), :], bufs[half], sems.at[half]).wait()

    def expert_up(half):
        xs = _load_token_major(bufs[half], (), rows).astype(BF16)
        hg = jnp.dot(xs, wg_ref[0], preferred_element_type=F32)
        hu = jnp.dot(xs, wu_ref[0], preferred_element_type=F32)
        return (_silu(hg) * hu).astype(BF16)

    def expert_down(half, h):
        y = jnp.dot(h, wd_ref[0], preferred_element_type=F32)
        for g in range(FFN_G):
            _store_token_major(y_ref, (half * FFN_G + g, 0), y[g * cap:(g + 1) * cap])

    batch0 = pl.program_id(1) * (2 * FFN_G)

    @pl.when(step == 0)
    def _():
        for g in range(FFN_G):
            def issue(r, carry, g=g):
                row_copy(idx_ref, batch0, g, r, 0).start()
                return carry
            lax.fori_loop(0, cap, issue, 0)
        wait_half(0)

    for g in range(FFN_G):
        for r in range(cap):
            row_copy(idx_ref, batch0, g, r, 1).start()
    h0 = expert_up(0)
    expert_down(0, h0)
    wait_half(1)

    nxt_batch0 = (jnp.minimum(step + 1, last) % nb) * (2 * FFN_G)
    for g in range(FFN_G):
        for r in range(cap):
            row_copy(idx_next_ref, nxt_batch0, g, r, 0).start()
    h1 = expert_up(1)
    wait_half(0)
    expert_down(1, h1)


def _ffn(rows4, xn2, wg, wu, wd, cap):
    b = xn2.shape[0]
    n_e, d, _ = wg.shape
    blk = 2 * FFN_G
    nb = b // blk
    wspec = pl.BlockSpec((1, d, d), lambda e, bg: (e, 0, 0))

    def next_ids(e, bg):
        nxt = jnp.minimum(e * nb + bg + 1, n_e * nb - 1)
        return (nxt % nb, nxt // nb, 0, 0)

    return pl.pallas_call(
        functools.partial(_ffn_kernel, cap=cap),
        grid=(n_e, nb),
        in_specs=[
            pl.BlockSpec((blk, 1, 1, cap), lambda e, bg: (bg, e, 0, 0), memory_space=pltpu.SMEM),
            pl.BlockSpec((blk, 1, 1, cap), next_ids, memory_space=pltpu.SMEM),
            pl.BlockSpec(memory_space=pl.ANY),
            wspec, wspec, wspec,
        ],
        out_specs=pl.BlockSpec((blk, 1, cap * ROW_TILES, LANES), lambda e, bg: (bg, e, 0, 0)),
        out_shape=jax.ShapeDtypeStruct((b, n_e, cap * ROW_TILES, LANES), F32),
        scratch_shapes=[pltpu.VMEM((FFN_G * cap * ROW_TILES, LANES), F32),
                        pltpu.VMEM((FFN_G * cap * ROW_TILES, LANES), F32),
                        pltpu.SemaphoreType.DMA((2,))],
        compiler_params=_params("arbitrary", "arbitrary"),
        name="ffn",
    )(rows4, rows4, xn2, wg, wu, wd)


def _combine_kernel(row_ref, gate_ref, y_ref, ff_ref, *, cap):
    @pl.when(pl.program_id(1) == 0)
    def _():
        ff_ref[...] = jnp.zeros_like(ff_ref)

    for e in range(COMBINE_EXPERTS):
        for r0 in range(0, cap, COMBINE_UNROLL):
            dst = [pl.ds(pl.multiple_of(row_ref[0, e, 0, r0 + u], ROW_TILES), ROW_TILES)
                   for u in range(COMBINE_UNROLL)]
            sums = [ff_ref[0, dst[u], :]
                    + y_ref[0, e, (r0 + u) * ROW_TILES:(r0 + u + 1) * ROW_TILES, :] * gate_ref[0, e, 0, r0 + u]
                    for u in range(COMBINE_UNROLL)]
            for u in range(COMBINE_UNROLL):
                ff_ref[0, dst[u], :] = sums[u]


def _combine(rows4, gate4, y, t, cap):
    b, n_e = y.shape[:2]
    scalars = pl.BlockSpec((1, COMBINE_EXPERTS, 1, cap), lambda bi, e: (bi, e, 0, 0), memory_space=pltpu.SMEM)
    return pl.pallas_call(
        functools.partial(_combine_kernel, cap=cap),
        grid=(b, n_e // COMBINE_EXPERTS),
        in_specs=[scalars, scalars,
                  pl.BlockSpec((1, COMBINE_EXPERTS, cap * ROW_TILES, LANES), lambda bi, e: (bi, e, 0, 0))],
        out_specs=pl.BlockSpec((1, t * ROW_TILES, LANES), lambda bi, e: (bi, 0, 0)),
        out_shape=jax.ShapeDtypeStruct((b, t * ROW_TILES, LANES), F32),
        compiler_params=_params("parallel", "arbitrary"),
        name="combine",
    )(rows4, gate4, y)


def _final_kernel(h_ref, ff_ref, mod_ref, g_ref, o_ref):
    g2 = mod_ref[0, :, 5 * D_MODEL:6 * D_MODEL]
    ff = _load_token_major(ff_ref, (0,), h_ref.shape[1])
    o_ref[0] = h_ref[0] + g2 * _rms(ff, g_ref[...])


def _final(h1, ff, mod3, g):
    b, t, d = h1.shape
    tile = pl.BlockSpec((1, TM, d), lambda bi, i: (bi, i, 0))
    return pl.pallas_call(
        _final_kernel,
        grid=(b, t // TM),
        in_specs=[tile, pl.BlockSpec((1, TM * ROW_TILES, LANES), lambda bi, i: (bi, i, 0)),
                  pl.BlockSpec((1, 1, 6 * d), lambda bi, i: (bi, 0, 0)),
                  pl.BlockSpec((1, d), lambda bi, i: (0, 0))],
        out_specs=tile,
        out_shape=jax.ShapeDtypeStruct((b, t, d), F32),
        compiler_params=_params("parallel", "parallel"),
        name="final",
    )(h1, ff, mod3, g)


def _rope_tables(t):
    rows = t // GRID_W
    row = jnp.repeat(jnp.arange(rows, dtype=F32), GRID_W)
    col = jnp.tile(jnp.arange(GRID_W, dtype=F32), rows)
    inv = ROPE_BASE ** (-jnp.arange(0, ROT_AXIS, 2, dtype=F32) / ROT_AXIS)
    ang_r = row[:, None] * inv[None, :]
    ang_c = col[:, None] * inv[None, :]
    ang = jnp.concatenate([ang_r, ang_r, ang_c, ang_c], axis=-1)
    ang = jnp.concatenate([ang, ang], axis=-1)
    cos, sin = jnp.cos(ang), jnp.sin(ang)
    low = (jnp.arange(LANES) % ROT_AXIS) < ROT_AXIS // 2
    sin_lo = jnp.where(low[None, :], -sin, 0.0)
    sin_hi = jnp.where(low[None, :], 0.0, sin)
    scale = jnp.float32(math.log2(math.e) / math.sqrt(HEAD_DIM))
    return jnp.concatenate([cos * scale, sin_lo * scale, sin_hi * scale, cos, sin_lo, sin_hi], axis=-1)


def kernel(x, c, ctx, c_ctx, w_ada, b_ada, norm_pre_mix, norm_post_mix, norm_pre_ffn, norm_post_ffn,
           w_in, conv_w, lambda_q1, lambda_k1, lambda_q2, lambda_k2, subln_g, w_out, w_router,
           w_gate, w_up, w_down):
    b, t, d = x.shape
    depth = w_ada.shape[0]
    assert depth == 1 and d == D_MODEL and t % TM == 0 and t % TQ_BLOCK == 0 and TQ_BLOCK % TQ == 0 and b % (2 * FFN_G) == 0
    assert b + 1 <= MOD_ROWS
    layer = 0
    lam_init = 0.8 - 0.6 * math.exp(-0.3 * layer)
    cap = CAP_FACTOR * t // N_EXPERTS

    cs = jnp.concatenate([c, c_ctx[None, :], jnp.zeros((MOD_ROWS - b - 1, d), F32)], axis=0)
    mod3 = _ada(cs, w_ada[layer], b_ada[layer][None, :]).reshape(MOD_ROWS, 1, 6 * d)

    w_in_bf = w_in[layer].astype(BF16)
    q, k, v, conv = _inproj(x, mod3, norm_pre_mix[layer][None, :], w_in_bf, conv_w[layer], _rope_tables(t))
    kc, vc = _ctxproj(ctx, mod3, norm_pre_mix[layer][None, :], w_in_bf)
    attn = _attn(q, k, v, kc, vc, lambda_q1[layer][None, :], lambda_k1[layer][None, :],
                 lambda_q2[layer][None, :], lambda_k2[layer][None, :], subln_g[layer][None, :], lam_init)
    h1, xn2, aff_t = _outproj(attn, conv, x, mod3, norm_post_mix[layer][None, :],
                              norm_pre_ffn[layer][None, :], w_out[layer].astype(BF16), w_router[layer].T)

    idx_p, gate_p = _topk(aff_t, cap)
    rows4 = (jnp.swapaxes(idx_p[:, :, :N_EXPERTS], 1, 2) * ROW_TILES).reshape(b, N_EXPERTS, 1, cap)
    gate4 = jnp.swapaxes(gate_p[:, :, :N_EXPERTS], 1, 2).reshape(b, N_EXPERTS, 1, cap)

    y = _ffn(rows4, xn2, w_gate[layer].astype(BF16), w_up[layer].astype(BF16),
             w_down[layer].astype(BF16), cap)
    ff = _combine(rows4, gate4, y, t, cap)
    return _final(h1, ff, mod3, norm_post_ffn[layer][None, :])
```

```python
import functools
import math

import jax
import jax.numpy as jnp
from jax import lax
from jax.experimental import pallas as pl
from jax.experimental.pallas import tpu as pltpu

F32 = jnp.float32
BF16 = jnp.bfloat16
I32 = jnp.int32

D_MODEL = 1024
CTX_LEN = 256
GRID_W = 64
HEADS = 4
HEAD_DIM = 64
V_DIM = 2 * HEAD_DIM
QK_W = HEADS * 2 * HEAD_DIM
V_W = HEADS * V_DIM
CONV_W = 512
PROJ_W = 2 * QK_W + V_W + 3 * CONV_W
ROPE_BASE = 10000.0
ROT_AXIS = HEAD_DIM // 2
N_EXPERTS = 16
CAP_FACTOR = 2
EPS = 1e-6
LANES = 128
ROW_TILES = D_MODEL // LANES
MOD_ROWS = 40

TM = 1024
TQ = 256
TQ_BLOCK = 512
FFN_G = 2
COMBINE_UNROLL = 8
COMBINE_EXPERTS = 4
VMEM_LIMIT = 48 * 1024 * 1024
HIGHEST = lax.Precision.HIGHEST
NT_DIMS = (((1,), (1,)), ((), ()))


def _params(*sem):
    return pltpu.CompilerParams(dimension_semantics=sem, vmem_limit_bytes=VMEM_LIMIT)


def _rms(x, g):
    return x * lax.rsqrt(jnp.mean(x * x, axis=-1, keepdims=True) + EPS) * g


def _silu(x):
    return x * jax.nn.sigmoid(x)


def _store_token_major(ref, lead, x):
    for j in range(ROW_TILES):
        ref[lead + (pl.ds(j, x.shape[0], stride=ROW_TILES), slice(None))] = x[:, j * LANES:(j + 1) * LANES]


def _load_token_major(ref, lead, rows):
    return jnp.concatenate([ref[lead + (pl.ds(j, rows, stride=ROW_TILES), slice(None))]
                            for j in range(ROW_TILES)], axis=1)


def _ada_kernel(cs_ref, w_ref, b_ref, o_ref):
    s = _silu(cs_ref[...])
    o_ref[...] = jnp.dot(s, w_ref[...], preferred_element_type=F32, precision=HIGHEST) + b_ref[...]


def _ada(cs, w, b):
    n = w.shape[1]
    tn = 512
    return pl.pallas_call(
        _ada_kernel,
        grid=(n // tn,),
        in_specs=[pl.BlockSpec((MOD_ROWS, D_MODEL), lambda j: (0, 0)),
                  pl.BlockSpec((D_MODEL, tn), lambda j: (0, j)),
                  pl.BlockSpec((1, tn), lambda j: (0, j))],
        out_specs=pl.BlockSpec((MOD_ROWS, tn), lambda j: (0, j)),
        out_shape=jax.ShapeDtypeStruct((MOD_ROWS, n), F32),
        compiler_params=_params("parallel"),
        name="ada",
    )(cs, w, b)


def _rope_store(p, tabs, out_ref):
    cos, sin_lo, sin_hi = tabs
    for j in range(QK_W // LANES):
        xj = p[:, j * LANES:(j + 1) * LANES]
        r = xj * cos + pltpu.roll(xj, LANES - 16, 1) * sin_lo + pltpu.roll(xj, 16, 1) * sin_hi
        out_ref[0, :, j * LANES:(j + 1) * LANES] = r.astype(BF16)


def _store_values(v_ref, pv):
    ones = jnp.ones((pv.shape[0], V_DIM), BF16)
    for h in range(HEADS):
        v_ref[0, :, 2 * h * V_DIM:(2 * h + 1) * V_DIM] = pv[:, h * V_DIM:(h + 1) * V_DIM].astype(BF16)
        v_ref[0, :, (2 * h + 1) * V_DIM:(2 * h + 2) * V_DIM] = ones


def _in_kernel(x_ref, xp_ref, xn_ref, mod_ref, g_ref, w_ref, cw_ref, tab_ref,
               q_ref, k_ref, v_ref, c_ref):
    i = pl.program_id(1)
    last = pl.num_programs(1) - 1
    g = g_ref[...]
    sh = mod_ref[0, :, 0:D_MODEL]
    sc = mod_ref[0, :, D_MODEL:2 * D_MODEL]

    def norm_mod(xt):
        return (_rms(xt, g) * (1.0 + sc) + sh).astype(BF16)

    def proj(xb, c0, width):
        return jnp.dot(xb, w_ref[:, c0:c0 + width], preferred_element_type=F32)

    xb = norm_mod(x_ref[0])
    tabs = [tab_ref[:, j * LANES:(j + 1) * LANES] for j in range(6)]
    _rope_store(proj(xb, 0, QK_W), tabs[0:3], q_ref)
    _rope_store(proj(xb, QK_W, QK_W), tabs[3:6], k_ref)
    _store_values(v_ref, proj(xb, 2 * QK_W, V_W))

    c0 = 2 * QK_W + V_W
    gb = proj(xb, c0, CONV_W)
    u = proj(xb, c0 + CONV_W, CONV_W) * proj(xb, c0 + 2 * CONV_W, CONV_W)
    xh = norm_mod(jnp.concatenate([xp_ref[0], xn_ref[0]], axis=0))
    ph = proj(xh, c0 + CONV_W, 2 * CONV_W)
    uh = ph[:, :CONV_W] * ph[:, CONV_W:]
    u_before = jnp.where(i > 0, uh[7:8], 0.0)
    u_after = jnp.where(i < last, uh[8:9], 0.0)
    rows = lax.broadcasted_iota(I32, u.shape, 0)
    u_prev = jnp.where(rows == 0, u_before, pltpu.roll(u, 1, 0))
    u_next = jnp.where(rows == TM - 1, u_after, pltpu.roll(u, TM - 1, 0))
    cw = cw_ref[...]
    y = cw[0:1] * u_prev + cw[1:2] * u + cw[2:3] * u_next
    c_ref[0] = (gb * y).astype(BF16)


def _inproj(x, mod3, g, w_bf, conv_w, tab):
    b, t, _ = x.shape
    nt = t // TM
    hb = TM // 8
    out = jax.ShapeDtypeStruct((b, t, QK_W), BF16)
    tile = pl.BlockSpec((1, TM, QK_W), lambda bi, i: (bi, i, 0))
    return pl.pallas_call(
        _in_kernel,
        grid=(b, nt),
        in_specs=[
            pl.BlockSpec((1, TM, D_MODEL), lambda bi, i: (bi, i, 0)),
            pl.BlockSpec((1, 8, D_MODEL), lambda bi, i: (bi, jnp.maximum(i * hb - 1, 0), 0)),
            pl.BlockSpec((1, 8, D_MODEL), lambda bi, i: (bi, jnp.minimum((i + 1) * hb, t // 8 - 1), 0)),
            pl.BlockSpec((1, 1, 6 * D_MODEL), lambda bi, i: (bi, 0, 0)),
            pl.BlockSpec((1, D_MODEL), lambda bi, i: (0, 0)),
            pl.BlockSpec((D_MODEL, PROJ_W), lambda bi, i: (0, 0)),
            pl.BlockSpec((3, CONV_W), lambda bi, i: (0, 0)),
            pl.BlockSpec((TM, 6 * LANES), lambda bi, i: (i, 0)),
        ],
        out_specs=[tile, tile, pl.BlockSpec((1, TM, 2 * V_W), lambda bi, i: (bi, i, 0)), tile],
        out_shape=[out, out, jax.ShapeDtypeStruct((b, t, 2 * V_W), BF16), out],
        compiler_params=_params("parallel", "parallel"),
        name="inproj",
    )(x, x, x, mod3, g, w_bf, conv_w, tab)


def _ctx_kernel(x_ref, mod_ref, g_ref, wk_ref, wv_ref, k_ref, v_ref):
    sh = mod_ref[0, :, 0:D_MODEL]
    sc = mod_ref[0, :, D_MODEL:2 * D_MODEL]
    xb = (_rms(x_ref[0], g_ref[...]) * (1.0 + sc) + sh).astype(BF16)
    k_ref[0] = jnp.dot(xb, wk_ref[...], preferred_element_type=F32).astype(BF16)
    _store_values(v_ref, jnp.dot(xb, wv_ref[...], preferred_element_type=F32))


def _ctxproj(ctx, mod3, g, w_bf):
    b = ctx.shape[0]
    out = jax.ShapeDtypeStruct((b, CTX_LEN, QK_W), BF16)
    tile = pl.BlockSpec((1, CTX_LEN, QK_W), lambda bi: (bi, 0, 0))
    return pl.pallas_call(
        _ctx_kernel,
        grid=(b,),
        in_specs=[
            pl.BlockSpec((1, CTX_LEN, D_MODEL), lambda bi: (bi, 0, 0)),
            pl.BlockSpec((1, 1, 6 * D_MODEL), lambda bi: (b, 0, 0)),
            pl.BlockSpec((1, D_MODEL), lambda bi: (0, 0)),
            pl.BlockSpec((D_MODEL, QK_W), lambda bi: (0, 1)),
            pl.BlockSpec((D_MODEL, V_W), lambda bi: (0, 2)),
        ],
        out_specs=[tile, pl.BlockSpec((1, CTX_LEN, 2 * V_W), lambda bi: (bi, 0, 0))],
        out_shape=[out, jax.ShapeDtypeStruct((b, CTX_LEN, 2 * V_W), BF16)],
        compiler_params=_params("parallel"),
        name="ctxproj",
    )(ctx, mod3, g, w_bf, w_bf)


def _attn_kernel(q_ref, k_ref, v_ref, kc_ref, vc_ref, lq1_ref, lk1_ref, lq2_ref, lk2_ref,
                 sg_ref, o_ref, *, lam_init):
    lam = (jnp.exp(jnp.sum(lq1_ref[...] * lk1_ref[...], keepdims=True))
           - jnp.exp(jnp.sum(lq2_ref[...] * lk2_ref[...], keepdims=True)) + lam_init)
    lane = lax.broadcasted_iota(I32, (1, V_DIM), 1)
    first = (lane < HEAD_DIM).astype(BF16)
    second = (lane >= HEAD_DIM).astype(BF16)
    sg = sg_ref[...] * (1.0 - lam_init)

    for h, u in [(h, u) for u in range(TQ_BLOCK // TQ) for h in range(HEADS)]:
        sl = slice(h * V_DIM, (h + 1) * V_DIM)
        sv = slice(h * 2 * V_DIM, (h + 1) * 2 * V_DIM)
        rows = slice(u * TQ, (u + 1) * TQ)
        qh = q_ref[0, rows, sl]
        q2 = jnp.concatenate([qh * first, qh * second], axis=0)
        s_l = lax.dot_general(q2, k_ref[0, :, sl], NT_DIMS, preferred_element_type=F32)
        s_c = lax.dot_general(q2, kc_ref[0, :, sl], NT_DIMS, preferred_element_type=F32)
        m = jnp.maximum(jnp.max(s_l, axis=1, keepdims=True), jnp.max(s_c, axis=1, keepdims=True))
        p_l = jnp.exp2(s_l - m).astype(BF16)
        p_c = jnp.exp2(s_c - m).astype(BF16)
        pv = (jnp.dot(p_l, v_ref[0, :, sv], preferred_element_type=F32)
              + jnp.dot(p_c, vc_ref[0, :, sv], preferred_element_type=F32))
        num = pv[:, :V_DIM]
        den = pv[:, V_DIM:V_DIM + 1]
        o = num[:TQ] * (1.0 / den[:TQ]) - num[TQ:] * (lam / den[TQ:])
        o_ref[0, rows, sl] = _rms(o, sg).astype(BF16)


def _attn(q, k, v, kc, vc, lq1, lk1, lq2, lk2, sg, lam_init):
    b, t, _ = q.shape
    qt = pl.BlockSpec((1, TQ_BLOCK, QK_W), lambda bi, i: (bi, i, 0))
    vec = pl.BlockSpec((1, HEAD_DIM), lambda bi, i: (0, 0))

    def per_batch(rows, width):
        return pl.BlockSpec((1, rows, width), lambda bi, i: (bi, 0, 0))

    return pl.pallas_call(
        functools.partial(_attn_kernel, lam_init=lam_init),
        grid=(b, t // TQ_BLOCK),
        in_specs=[qt, per_batch(t, QK_W), per_batch(t, 2 * V_W),
                  per_batch(CTX_LEN, QK_W), per_batch(CTX_LEN, 2 * V_W),
                  vec, vec, vec, vec, pl.BlockSpec((1, V_DIM), lambda bi, i: (0, 0))],
        out_specs=qt,
        out_shape=jax.ShapeDtypeStruct((b, t, V_W), BF16),
        compiler_params=_params("parallel", "parallel"),
        name="attn",
    )(q, k, v, kc, vc, lq1, lk1, lq2, lk2, sg)


def _out_kernel(a_ref, c_ref, x_ref, mod_ref, gpost_ref, gpre_ref, wo_ref, wr_ref,
                h_ref, xn_ref, aff_ref):
    mix = (jnp.dot(a_ref[0], wo_ref[0:V_W, :], preferred_element_type=F32)
           + jnp.dot(c_ref[0], wo_ref[V_W:V_W + CONV_W, :], preferred_element_type=F32))
    g1 = mod_ref[0, :, 2 * D_MODEL:3 * D_MODEL]
    sh2 = mod_ref[0, :, 3 * D_MODEL:4 * D_MODEL]
    sc2 = mod_ref[0, :, 4 * D_MODEL:5 * D_MODEL]
    h1 = x_ref[0] + g1 * _rms(mix, gpost_ref[...])
    h_ref[0] = h1
    xn2 = _rms(h1, gpre_ref[...]) * (1.0 + sc2) + sh2
    _store_token_major(xn_ref, (0,), xn2)
    def split(a):
        hi = a.astype(BF16)
        return hi, (a - hi.astype(F32)).astype(BF16)

    w_hi, w_lo = split(wr_ref[...])
    x_hi, x_lo = split(xn2)
    both = lax.dot_general(jnp.concatenate([w_hi, w_lo], axis=0), x_hi, NT_DIMS,
                           preferred_element_type=F32)
    logits = (both[:N_EXPERTS] + both[N_EXPERTS:]
              + lax.dot_general(w_hi, x_lo, NT_DIMS, preferred_element_type=F32))
    e = jnp.exp(logits - jnp.max(logits, axis=0, keepdims=True))
    aff_ref[0] = e / jnp.sum(e, axis=0, keepdims=True)


def _outproj(attn, conv, x, mod3, gpost, gpre, wo_bf, wr_t):
    b, t, _ = x.shape
    half = pl.BlockSpec((1, TM, V_W), lambda bi, i: (bi, i, 0))
    tile = pl.BlockSpec((1, TM, D_MODEL), lambda bi, i: (bi, i, 0))
    row = pl.BlockSpec((1, D_MODEL), lambda bi, i: (0, 0))
    act = jax.ShapeDtypeStruct((b, t, D_MODEL), F32)
    return pl.pallas_call(
        _out_kernel,
        grid=(b, t // TM),
        in_specs=[half, half, tile,
                  pl.BlockSpec((1, 1, 6 * D_MODEL), lambda bi, i: (bi, 0, 0)),
                  row, row,
                  pl.BlockSpec((V_W + CONV_W, D_MODEL), lambda bi, i: (0, 0)),
                  pl.BlockSpec((N_EXPERTS, D_MODEL), lambda bi, i: (0, 0))],
        out_specs=[tile, pl.BlockSpec((1, TM * ROW_TILES, LANES), lambda bi, i: (bi, i, 0)),
                   pl.BlockSpec((1, N_EXPERTS, TM), lambda bi, i: (bi, 0, i))],
        out_shape=[act, jax.ShapeDtypeStruct((b, t * ROW_TILES, LANES), F32),
                   jax.ShapeDtypeStruct((b, N_EXPERTS, t), F32)],
        compiler_params=_params("parallel", "parallel"),
        name="outproj",
    )(attn, conv, x, mod3, gpost, gpre, wo_bf, wr_t)


def _prefix_exclusive(mask_f):
    r = lax.broadcasted_iota(I32, (LANES, LANES), 0)
    c = lax.broadcasted_iota(I32, (LANES, LANES), 1)
    upper = (r <= c).astype(BF16)
    off = jnp.zeros((mask_f.shape[0], 1), F32)
    outs = []
    for j in range(mask_f.shape[1] // LANES):
        blk = mask_f[:, j * LANES:(j + 1) * LANES]
        inc = jnp.dot(blk.astype(BF16), upper, preferred_element_type=F32)
        outs.append(inc - blk + off)
        off = off + inc[:, LANES - 1:LANES]
    return jnp.concatenate(outs, axis=1)


def _topk_kernel(aff_ref, idx_ref, gate_ref, *, cap):
    aff = aff_ref[0]
    n_e, t = aff.shape

    def search(i, thr_bits):
        cand = thr_bits | (jnp.int32(1) << (29 - i))
        cnt = jnp.sum((aff >= pltpu.bitcast(cand, F32)).astype(F32), axis=1, keepdims=True)
        return jnp.where(cnt >= cap, cand, thr_bits)

    thr = pltpu.bitcast(lax.fori_loop(0, 30, search, jnp.zeros((n_e, 1), I32)), F32)
    gt = aff > thr
    eq = aff == thr
    need = cap - jnp.sum(gt.astype(F32), axis=1, keepdims=True)
    eq_f = eq.astype(F32)
    take_eq = jnp.where(_prefix_exclusive(eq_f) < need, eq_f, 0.0)
    sel_f = jnp.where(gt, 1.0, take_eq)
    slot = jnp.where(sel_f > 0.0, _prefix_exclusive(sel_f), -1.0)

    tok = lax.broadcasted_iota(I32, (cap, t), 1).astype(F32)
    slot_ids = lax.broadcasted_iota(I32, (cap, 1), 0).astype(F32)
    lane = lax.broadcasted_iota(I32, (cap, LANES), 1)
    idx_acc = jnp.zeros((cap, LANES), F32)
    gate_acc = jnp.zeros((cap, LANES), F32)
    for e in range(n_e):
        hit = slot[e:e + 1, :] == slot_ids
        idx_col = jnp.sum(jnp.where(hit, tok, 0.0), axis=1, keepdims=True)
        gate_col = jnp.sum(jnp.where(hit, aff[e:e + 1, :], 0.0), axis=1, keepdims=True)
        idx_acc = jnp.where(lane == e, idx_col, idx_acc)
        gate_acc = jnp.where(lane == e, gate_col, gate_acc)
    idx_ref[0] = idx_acc.astype(I32)
    gate_ref[0] = gate_acc


def _topk(aff_t, cap):
    b, n_e, t = aff_t.shape
    out = pl.BlockSpec((1, cap, LANES), lambda bi: (bi, 0, 0))
    return pl.pallas_call(
        functools.partial(_topk_kernel, cap=cap),
        grid=(b,),
        in_specs=[pl.BlockSpec((1, n_e, t), lambda bi: (bi, 0, 0))],
        out_specs=[out, out],
        out_shape=[jax.ShapeDtypeStruct((b, cap, LANES), I32),
                   jax.ShapeDtypeStruct((b, cap, LANES), F32)],
        compiler_params=_params("parallel"),
        name="topk",
    )(aff_t)


def _ffn_kernel(idx_ref, idx_next_ref, xn_hbm, wg_ref, wu_ref, wd_ref, y_ref,
                xs_a, xs_b, sems, fence_sem, *, cap):
    nb = pl.num_programs(1)
    step = pl.program_id(0) * nb + pl.program_id(1)
    last = pl.num_programs(0) * nb - 1
    rows = FFN_G * cap
    bufs = (xs_a, xs_b)

    def row_copy(ids_ref, batch0, g, r, half):
        src_row = pl.multiple_of(ids_ref[half * FFN_G + g, 0, 0, r], ROW_TILES)
        return pltpu.make_async_copy(xn_hbm.at[batch0 + half * FFN_G + g, pl.ds(src_row, ROW_TILES), :],
                                     bufs[half].at[pl.ds((g * cap + r) * ROW_TILES, ROW_TILES), :],
                                     sems.at[half])

    def wait_half(half):
        pltpu.make_async_copy(xn_hbm.at[0, pl.ds(0, rows * ROW_TILES), :], bufs[half], sems.at[half]).wait()

    def expert_up(half):
        xs = _load_token_major(bufs[half], (), rows)
        hg = jnp.dot(xs, wg_ref[0], preferred_element_type=F32, precision=lax.Precision.DEFAULT)
        hu = jnp.dot(xs, wu_ref[0], preferred_element_type=F32, precision=lax.Precision.DEFAULT)
        return _silu(hg) * hu

    def expert_down(half, h):
        y = jnp.dot(h, wd_ref[0], preferred_element_type=F32, precision=lax.Precision.DEFAULT)
        for g in range(FFN_G):
            _store_token_major(y_ref, (half * FFN_G + g, 0), y[g * cap:(g + 1) * cap])

    batch0 = pl.program_id(1) * (2 * FFN_G)

    @pl.when(step == 0)
    def _():
        for g in range(FFN_G):
            def issue(r, carry, g=g):
                row_copy(idx_ref, batch0, g, r, 0).start()
                return carry
            lax.fori_loop(0, cap, issue, 0)
        wait_half(0)

    for g in range(FFN_G):
        for r in range(cap):
            row_copy(idx_ref, batch0, g, r, 1).start()
    h0 = expert_up(0)
    wait_half(1)
    expert_down(0, h0)
    h1 = expert_up(1)

    nxt_batch0 = (jnp.minimum(step + 1, last) % nb) * (2 * FFN_G)
    for g in range(FFN_G):
        for r in range(cap):
            row_copy(idx_next_ref, nxt_batch0, g, r, 0).start()
    pl.semaphore_signal(fence_sem, 1)
    pl.semaphore_wait(fence_sem, 1)
    expert_down(1, h1)
    wait_half(0)


def _ffn(rows4, xn2, wg, wu, wd, cap):
    b = xn2.shape[0]
    n_e, d, _ = wg.shape
    blk = 2 * FFN_G
    nb = b // blk
    wspec = pl.BlockSpec((1, d, d), lambda e, bg: (e, 0, 0))

    def next_ids(e, bg):
        nxt = jnp.minimum(e * nb + bg + 1, n_e * nb - 1)
        return (nxt % nb, nxt // nb, 0, 0)

    return pl.pallas_call(
        functools.partial(_ffn_kernel, cap=cap),
        grid=(n_e, nb),
        in_specs=[
            pl.BlockSpec((blk, 1, 1, cap), lambda e, bg: (bg, e, 0, 0), memory_space=pltpu.SMEM),
            pl.BlockSpec((blk, 1, 1, cap), next_ids, memory_space=pltpu.SMEM),
            pl.BlockSpec(memory_space=pl.ANY),
            wspec, wspec, wspec,
        ],
        out_specs=pl.BlockSpec((blk, 1, cap * ROW_TILES, LANES), lambda e, bg: (bg, e, 0, 0)),
        out_shape=jax.ShapeDtypeStruct((b, n_e, cap * ROW_TILES, LANES), F32),
        scratch_shapes=[pltpu.VMEM((FFN_G * cap * ROW_TILES, LANES), F32),
                        pltpu.VMEM((FFN_G * cap * ROW_TILES, LANES), F32),
                        pltpu.SemaphoreType.DMA((2,)), pltpu.SemaphoreType.REGULAR],
        compiler_params=_params("arbitrary", "arbitrary"),
        name="ffn",
    )(rows4, rows4, xn2, wg, wu, wd)


def _combine_kernel(row_ref, gate_ref, y_ref, ff_ref, *, cap):
    @pl.when(pl.program_id(1) == 0)
    def _():
        ff_ref[...] = jnp.zeros_like(ff_ref)

    for e in range(COMBINE_EXPERTS):
        for r0 in range(0, cap, COMBINE_UNROLL):
            dst = [pl.ds(pl.multiple_of(row_ref[0, e, 0, r0 + u], ROW_TILES), ROW_TILES)
                   for u in range(COMBINE_UNROLL)]
            sums = [ff_ref[0, dst[u], :]
                    + y_ref[0, e, (r0 + u) * ROW_TILES:(r0 + u + 1) * ROW_TILES, :] * gate_ref[0, e, 0, r0 + u]
                    for u in range(COMBINE_UNROLL)]
            for u in range(COMBINE_UNROLL):
                ff_ref[0, dst[u], :] = sums[u]


def _combine(rows4, gate4, y, t, cap):
    b, n_e = y.shape[:2]
    scalars = pl.BlockSpec((1, COMBINE_EXPERTS, 1, cap), lambda bi, e: (bi, e, 0, 0), memory_space=pltpu.SMEM)
    return pl.pallas_call(
        functools.partial(_combine_kernel, cap=cap),
        grid=(b, n_e // COMBINE_EXPERTS),
        in_specs=[scalars, scalars,
                  pl.BlockSpec((1, COMBINE_EXPERTS, cap * ROW_TILES, LANES), lambda bi, e: (bi, e, 0, 0))],
        out_specs=pl.BlockSpec((1, t * ROW_TILES, LANES), lambda bi, e: (bi, 0, 0)),
        out_shape=jax.ShapeDtypeStruct((b, t * ROW_TILES, LANES), F32),
        compiler_params=_params("parallel", "arbitrary"),
        name="combine",
    )(rows4, gate4, y)


def _final_kernel(h_ref, ff_ref, mod_ref, g_ref, o_ref):
    g2 = mod_ref[0, :, 5 * D_MODEL:6 * D_MODEL]
    ff = _load_token_major(ff_ref, (0,), h_ref.shape[1])
    o_ref[0] = h_ref[0] + g2 * _rms(ff, g_ref[...])


def _final(h1, ff, mod3, g):
    b, t, d = h1.shape
    tile = pl.BlockSpec((1, TM, d), lambda bi, i: (bi, i, 0))
    return pl.pallas_call(
        _final_kernel,
        grid=(b, t // TM),
        in_specs=[tile, pl.BlockSpec((1, TM * ROW_TILES, LANES), lambda bi, i: (bi, i, 0)),
                  pl.BlockSpec((1, 1, 6 * d), lambda bi, i: (bi, 0, 0)),
                  pl.BlockSpec((1, d), lambda bi, i: (0, 0))],
        out_specs=tile,
        out_shape=jax.ShapeDtypeStruct((b, t, d), F32),
        compiler_params=_params("parallel", "parallel"),
        name="final",
    )(h1, ff, mod3, g)


def _rope_tables(t):
    rows = t // GRID_W
    row = jnp.repeat(jnp.arange(rows, dtype=F32), GRID_W)
    col = jnp.tile(jnp.arange(GRID_W, dtype=F32), rows)
    inv = ROPE_BASE ** (-jnp.arange(0, ROT_AXIS, 2, dtype=F32) / ROT_AXIS)
    ang_r = row[:, None] * inv[None, :]
    ang_c = col[:, None] * inv[None, :]
    ang = jnp.concatenate([ang_r, ang_r, ang_c, ang_c], axis=-1)
    ang = jnp.concatenate([ang, ang], axis=-1)
    cos, sin = jnp.cos(ang), jnp.sin(ang)
    low = (jnp.arange(LANES) % ROT_AXIS) < ROT_AXIS // 2
    sin_lo = jnp.where(low[None, :], -sin, 0.0)
    sin_hi = jnp.where(low[None, :], 0.0, sin)
    scale = jnp.float32(math.log2(math.e) / math.sqrt(HEAD_DIM))
    return jnp.concatenate([cos * scale, sin_lo * scale, sin_hi * scale, cos, sin_lo, sin_hi], axis=-1)


def kernel(x, c, ctx, c_ctx, w_ada, b_ada, norm_pre_mix, norm_post_mix, norm_pre_ffn, norm_post_ffn,
           w_in, conv_w, lambda_q1, lambda_k1, lambda_q2, lambda_k2, subln_g, w_out, w_router,
           w_gate, w_up, w_down):
    b, t, d = x.shape
    depth = w_ada.shape[0]
    assert depth == 1 and d == D_MODEL and t % TM == 0 and t % TQ_BLOCK == 0 and TQ_BLOCK % TQ == 0 and b % (2 * FFN_G) == 0
    assert b + 1 <= MOD_ROWS
    layer = 0
    lam_init = 0.8 - 0.6 * math.exp(-0.3 * layer)
    cap = CAP_FACTOR * t // N_EXPERTS

    cs = jnp.concatenate([c, c_ctx[None, :], jnp.zeros((MOD_ROWS - b - 1, d), F32)], axis=0)
    mod3 = _ada(cs, w_ada[layer], b_ada[layer][None, :]).reshape(MOD_ROWS, 1, 6 * d)

    w_in_bf = w_in[layer].astype(BF16)
    q, k, v, conv = _inproj(x, mod3, norm_pre_mix[layer][None, :], w_in_bf, conv_w[layer], _rope_tables(t))
    kc, vc = _ctxproj(ctx, mod3, norm_pre_mix[layer][None, :], w_in_bf)
    attn = _attn(q, k, v, kc, vc, lambda_q1[layer][None, :], lambda_k1[layer][None, :],
                 lambda_q2[layer][None, :], lambda_k2[layer][None, :], subln_g[layer][None, :], lam_init)
    h1, xn2, aff_t = _outproj(attn, conv, x, mod3, norm_post_mix[layer][None, :],
                              norm_pre_ffn[layer][None, :], w_out[layer].astype(BF16), w_router[layer].T)

    idx_p, gate_p = _topk(aff_t, cap)
    rows4 = (jnp.swapaxes(idx_p[:, :, :N_EXPERTS], 1, 2) * ROW_TILES).reshape(b, N_EXPERTS, 1, cap)
    gate4 = jnp.swapaxes(gate_p[:, :, :N_EXPERTS], 1, 2).reshape(b, N_EXPERTS, 1, cap)

    y = _ffn(rows4, xn2, w_gate[layer], w_up[layer], w_down[layer], cap)
    ff = _combine(rows4, gate4, y, t, cap)
    return _final(h1, ff, mod3, norm_post_ffn[layer][None, :])
```

```python
import functools
import math

import jax
import jax.numpy as jnp
from jax import lax
from jax.experimental import pallas as pl
from jax.experimental.pallas import tpu as pltpu

F32 = jnp.float32
BF16 = jnp.bfloat16
I32 = jnp.int32

D_MODEL = 1024
CTX_LEN = 256
GRID_W = 64
HEADS = 4
HEAD_DIM = 64
V_DIM = 2 * HEAD_DIM
QK_W = HEADS * 2 * HEAD_DIM
V_W = HEADS * V_DIM
CONV_W = 512
PROJ_W = 2 * QK_W + V_W + 3 * CONV_W
ROPE_BASE = 10000.0
ROT_AXIS = HEAD_DIM // 2
N_EXPERTS = 16
CAP_FACTOR = 2
EPS = 1e-6
LANES = 128
ROW_TILES = D_MODEL // LANES
MOD_ROWS = 40

TM = 1024
TQ = 256
TQ_BLOCK = 512
FFN_G = 2
TOK_RADIX_BITS = 6
TOK_RADIX = 1 << TOK_RADIX_BITS
COMBINE_UNROLL = 8
COMBINE_EXPERTS = 4
FINAL_ROWS = 256
COMBINE_VMEM_LIMIT = 56 * 1024 * 1024
VMEM_LIMIT = 48 * 1024 * 1024
HIGHEST = lax.Precision.HIGHEST
NT_DIMS = (((1,), (1,)), ((), ()))


def _params(*sem):
    return pltpu.CompilerParams(dimension_semantics=sem, vmem_limit_bytes=VMEM_LIMIT)


def _rms(x, g):
    return x * lax.rsqrt(jnp.mean(x * x, axis=-1, keepdims=True) + EPS) * g


def _silu(x):
    return x * jax.nn.sigmoid(x)


def _store_token_major(ref, lead, x):
    for j in range(ROW_TILES):
        ref[lead + (pl.ds(j, x.shape[0], stride=ROW_TILES), slice(None))] = x[:, j * LANES:(j + 1) * LANES]


def _load_token_major(ref, lead, rows):
    return jnp.concatenate([ref[lead + (pl.ds(j, rows, stride=ROW_TILES), slice(None))]
                            for j in range(ROW_TILES)], axis=1)


def _ada_kernel(cs_ref, w_ref, b_ref, o_ref):
    s = _silu(cs_ref[...])
    o_ref[...] = jnp.dot(s, w_ref[...], preferred_element_type=F32, precision=HIGHEST) + b_ref[...]


def _ada(cs, w, b):
    n = w.shape[1]
    tn = 512
    return pl.pallas_call(
        _ada_kernel,
        grid=(n // tn,),
        in_specs=[pl.BlockSpec((MOD_ROWS, D_MODEL), lambda j: (0, 0)),
                  pl.BlockSpec((D_MODEL, tn), lambda j: (0, j)),
                  pl.BlockSpec((1, tn), lambda j: (0, j))],
        out_specs=pl.BlockSpec((MOD_ROWS, tn), lambda j: (0, j)),
        out_shape=jax.ShapeDtypeStruct((MOD_ROWS, n), F32),
        compiler_params=_params("parallel"),
        name="ada",
    )(cs, w, b)


def _rope_store(p, tabs, out_ref):
    cos, sin_lo, sin_hi = tabs
    for j in range(QK_W // LANES):
        xj = p[:, j * LANES:(j + 1) * LANES]
        r = xj * cos + pltpu.roll(xj, LANES - 16, 1) * sin_lo + pltpu.roll(xj, 16, 1) * sin_hi
        out_ref[0, :, j * LANES:(j + 1) * LANES] = r.astype(BF16)


def _store_values(v_ref, pv):
    ones = jnp.ones((pv.shape[0], V_DIM), BF16)
    for h in range(HEADS):
        v_ref[0, :, 2 * h * V_DIM:(2 * h + 1) * V_DIM] = pv[:, h * V_DIM:(h + 1) * V_DIM].astype(BF16)
        v_ref[0, :, (2 * h + 1) * V_DIM:(2 * h + 2) * V_DIM] = ones


def _in_kernel(x_ref, xp_ref, xn_ref, mod_ref, g_ref, w_ref, cw_ref, tab_ref,
               q_ref, k_ref, v_ref, c_ref):
    i = pl.program_id(1)
    last = pl.num_programs(1) - 1
    g = g_ref[...]
    sh = mod_ref[0, :, 0:D_MODEL]
    sc = mod_ref[0, :, D_MODEL:2 * D_MODEL]

    def norm_mod(xt):
        return (_rms(xt, g) * (1.0 + sc) + sh).astype(BF16)

    def proj(xb, c0, width):
        return jnp.dot(xb, w_ref[:, c0:c0 + width], preferred_element_type=F32)

    xb = norm_mod(x_ref[0])
    tabs = [tab_ref[:, j * LANES:(j + 1) * LANES] for j in range(6)]
    _rope_store(proj(xb, 0, QK_W), tabs[0:3], q_ref)
    _rope_store(proj(xb, QK_W, QK_W), tabs[3:6], k_ref)
    _store_values(v_ref, proj(xb, 2 * QK_W, V_W))

    c0 = 2 * QK_W + V_W
    gb = proj(xb, c0, CONV_W)
    u = proj(xb, c0 + CONV_W, CONV_W) * proj(xb, c0 + 2 * CONV_W, CONV_W)
    xh = norm_mod(jnp.concatenate([xp_ref[0], xn_ref[0]], axis=0))
    ph = proj(xh, c0 + CONV_W, 2 * CONV_W)
    uh = ph[:, :CONV_W] * ph[:, CONV_W:]
    u_before = jnp.where(i > 0, uh[7:8], 0.0)
    u_after = jnp.where(i < last, uh[8:9], 0.0)
    rows = lax.broadcasted_iota(I32, u.shape, 0)
    u_prev = jnp.where(rows == 0, u_before, pltpu.roll(u, 1, 0))
    u_next = jnp.where(rows == TM - 1, u_after, pltpu.roll(u, TM - 1, 0))
    cw = cw_ref[...]
    y = cw[0:1] * u_prev + cw[1:2] * u + cw[2:3] * u_next
    c_ref[0] = (gb * y).astype(BF16)


def _inproj(x, mod3, g, w_bf, conv_w, tab):
    b, t, _ = x.shape
    nt = t // TM
    hb = TM // 8
    out = jax.ShapeDtypeStruct((b, t, QK_W), BF16)
    tile = pl.BlockSpec((1, TM, QK_W), lambda bi, i: (bi, i, 0))
    return pl.pallas_call(
        _in_kernel,
        grid=(b, nt),
        in_specs=[
            pl.BlockSpec((1, TM, D_MODEL), lambda bi, i: (bi, i, 0)),
            pl.BlockSpec((1, 8, D_MODEL), lambda bi, i: (bi, jnp.maximum(i * hb - 1, 0), 0)),
            pl.BlockSpec((1, 8, D_MODEL), lambda bi, i: (bi, jnp.minimum((i + 1) * hb, t // 8 - 1), 0)),
            pl.BlockSpec((1, 1, 6 * D_MODEL), lambda bi, i: (bi, 0, 0)),
            pl.BlockSpec((1, D_MODEL), lambda bi, i: (0, 0)),
            pl.BlockSpec((D_MODEL, PROJ_W), lambda bi, i: (0, 0)),
            pl.BlockSpec((3, CONV_W), lambda bi, i: (0, 0)),
            pl.BlockSpec((TM, 6 * LANES), lambda bi, i: (i, 0)),
        ],
        out_specs=[tile, tile, pl.BlockSpec((1, TM, 2 * V_W), lambda bi, i: (bi, i, 0)), tile],
        out_shape=[out, out, jax.ShapeDtypeStruct((b, t, 2 * V_W), BF16), out],
        compiler_params=_params("parallel", "parallel"),
        name="inproj",
    )(x, x, x, mod3, g, w_bf, conv_w, tab)


def _ctx_kernel(x_ref, mod_ref, g_ref, wk_ref, wv_ref, k_ref, v_ref):
    sh = mod_ref[0, :, 0:D_MODEL]
    sc = mod_ref[0, :, D_MODEL:2 * D_MODEL]
    xb = (_rms(x_ref[0], g_ref[...]) * (1.0 + sc) + sh).astype(BF16)
    k_ref[0] = jnp.dot(xb, wk_ref[...], preferred_element_type=F32).astype(BF16)
    _store_values(v_ref, jnp.dot(xb, wv_ref[...], preferred_element_type=F32))


def _ctxproj(ctx, mod3, g, w_bf):
    b = ctx.shape[0]
    out = jax.ShapeDtypeStruct((b, CTX_LEN, QK_W), BF16)
    tile = pl.BlockSpec((1, CTX_LEN, QK_W), lambda bi: (bi, 0, 0))
    return pl.pallas_call(
        _ctx_kernel,
        grid=(b,),
        in_specs=[
            pl.BlockSpec((1, CTX_LEN, D_MODEL), lambda bi: (bi, 0, 0)),
            pl.BlockSpec((1, 1, 6 * D_MODEL), lambda bi: (b, 0, 0)),
            pl.BlockSpec((1, D_MODEL), lambda bi: (0, 0)),
            pl.BlockSpec((D_MODEL, QK_W), lambda bi: (0, 1)),
            pl.BlockSpec((D_MODEL, V_W), lambda bi: (0, 2)),
        ],
        out_specs=[tile, pl.BlockSpec((1, CTX_LEN, 2 * V_W), lambda bi: (bi, 0, 0))],
        out_shape=[out, jax.ShapeDtypeStruct((b, CTX_LEN, 2 * V_W), BF16)],
        compiler_params=_params("parallel"),
        name="ctxproj",
    )(ctx, mod3, g, w_bf, w_bf)


def _attn_kernel(q_ref, k_ref, v_ref, kc_ref, vc_ref, lq1_ref, lk1_ref, lq2_ref, lk2_ref,
                 sg_ref, o_ref, *, lam_init):
    lam = (jnp.exp(jnp.sum(lq1_ref[...] * lk1_ref[...], keepdims=True))
           - jnp.exp(jnp.sum(lq2_ref[...] * lk2_ref[...], keepdims=True)) + lam_init)
    lane = lax.broadcasted_iota(I32, (1, V_DIM), 1)
    first = (lane < HEAD_DIM).astype(BF16)
    second = (lane >= HEAD_DIM).astype(BF16)
    sg = sg_ref[...] * (1.0 - lam_init)

    for h, u in [(h, u) for u in range(TQ_BLOCK // TQ) for h in range(HEADS)]:
        sl = slice(h * V_DIM, (h + 1) * V_DIM)
        sv = slice(h * 2 * V_DIM, (h + 1) * 2 * V_DIM)
        rows = slice(u * TQ, (u + 1) * TQ)
        qh = q_ref[0, rows, sl]
        q2 = jnp.concatenate([qh * first, qh * second], axis=0)
        s_l = lax.dot_general(q2, k_ref[0, :, sl], NT_DIMS, preferred_element_type=F32)
        s_c = lax.dot_general(q2, kc_ref[0, :, sl], NT_DIMS, preferred_element_type=F32)
        m = jnp.maximum(jnp.max(s_l, axis=1, keepdims=True), jnp.max(s_c, axis=1, keepdims=True))
        p_l = jnp.exp2(s_l - m).astype(BF16)
        p_c = jnp.exp2(s_c - m).astype(BF16)
        pv = (jnp.dot(p_l, v_ref[0, :, sv], preferred_element_type=F32)
              + jnp.dot(p_c, vc_ref[0, :, sv], preferred_element_type=F32))
        num = pv[:, :V_DIM]
        den = pv[:, V_DIM:V_DIM + 1]
        o = num[:TQ] * (1.0 / den[:TQ]) - num[TQ:] * (lam / den[TQ:])
        o_ref[0, rows, sl] = _rms(o, sg).astype(BF16)


def _attn(q, k, v, kc, vc, lq1, lk1, lq2, lk2, sg, lam_init):
    b, t, _ = q.shape
    qt = pl.BlockSpec((1, TQ_BLOCK, QK_W), lambda bi, i: (bi, i, 0))
    vec = pl.BlockSpec((1, HEAD_DIM), lambda bi, i: (0, 0))

    def per_batch(rows, width):
        return pl.BlockSpec((1, rows, width), lambda bi, i: (bi, 0, 0))

    return pl.pallas_call(
        functools.partial(_attn_kernel, lam_init=lam_init),
        grid=(b, t // TQ_BLOCK),
        in_specs=[qt, per_batch(t, QK_W), per_batch(t, 2 * V_W),
                  per_batch(CTX_LEN, QK_W), per_batch(CTX_LEN, 2 * V_W),
                  vec, vec, vec, vec, pl.BlockSpec((1, V_DIM), lambda bi, i: (0, 0))],
        out_specs=qt,
        out_shape=jax.ShapeDtypeStruct((b, t, V_W), BF16),
        compiler_params=_params("parallel", "parallel"),
        name="attn",
    )(q, k, v, kc, vc, lq1, lk1, lq2, lk2, sg)


def _out_kernel(a_ref, c_ref, x_ref, mod_ref, gpost_ref, gpre_ref, wo_ref, wr_ref,
                h_ref, xn_ref, aff_ref):
    mix = (jnp.dot(a_ref[0], wo_ref[0:V_W, :], preferred_element_type=F32)
           + jnp.dot(c_ref[0], wo_ref[V_W:V_W + CONV_W, :], preferred_element_type=F32))
    g1 = mod_ref[0, :, 2 * D_MODEL:3 * D_MODEL]
    sh2 = mod_ref[0, :, 3 * D_MODEL:4 * D_MODEL]
    sc2 = mod_ref[0, :, 4 * D_MODEL:5 * D_MODEL]
    h1 = x_ref[0] + g1 * _rms(mix, gpost_ref[...])
    h_ref[0] = h1
    xn2 = _rms(h1, gpre_ref[...]) * (1.0 + sc2) + sh2
    _store_token_major(xn_ref, (0,), xn2)
    def split(a):
        hi = a.astype(BF16)
        return hi, (a - hi.astype(F32)).astype(BF16)

    w_hi, w_lo = split(wr_ref[...])
    x_hi, x_lo = split(xn2)
    both = lax.dot_general(jnp.concatenate([w_hi, w_lo], axis=0), x_hi, NT_DIMS,
                           preferred_element_type=F32)
    logits = (both[:N_EXPERTS] + both[N_EXPERTS:]
              + lax.dot_general(w_hi, x_lo, NT_DIMS, preferred_element_type=F32))
    e = jnp.exp(logits - jnp.max(logits, axis=0, keepdims=True))
    aff_ref[0] = e / jnp.sum(e, axis=0, keepdims=True)


def _outproj(attn, conv, x, mod3, gpost, gpre, wo_bf, wr_t):
    b, t, _ = x.shape
    half = pl.BlockSpec((1, TM, V_W), lambda bi, i: (bi, i, 0))
    tile = pl.BlockSpec((1, TM, D_MODEL), lambda bi, i: (bi, i, 0))
    row = pl.BlockSpec((1, D_MODEL), lambda bi, i: (0, 0))
    act = jax.ShapeDtypeStruct((b, t, D_MODEL), F32)
    return pl.pallas_call(
        _out_kernel,
        grid=(b, t // TM),
        in_specs=[half, half, tile,
                  pl.BlockSpec((1, 1, 6 * D_MODEL), lambda bi, i: (bi, 0, 0)),
                  row, row,
                  pl.BlockSpec((V_W + CONV_W, D_MODEL), lambda bi, i: (0, 0)),
                  pl.BlockSpec((N_EXPERTS, D_MODEL), lambda bi, i: (0, 0))],
        out_specs=[tile, pl.BlockSpec((1, TM * ROW_TILES, LANES), lambda bi, i: (bi, i, 0)),
                   pl.BlockSpec((1, N_EXPERTS, TM), lambda bi, i: (bi, 0, i))],
        out_shape=[act, jax.ShapeDtypeStruct((b, t * ROW_TILES, LANES), F32),
                   jax.ShapeDtypeStruct((b, N_EXPERTS, t), F32)],
        compiler_params=_params("parallel", "parallel"),
        name="outproj",
    )(attn, conv, x, mod3, gpost, gpre, wo_bf, wr_t)


def _prefix_exclusive(mask_f):
    r = lax.broadcasted_iota(I32, (LANES, LANES), 0)
    c = lax.broadcasted_iota(I32, (LANES, LANES), 1)
    upper = (r <= c).astype(BF16)
    off = jnp.zeros((mask_f.shape[0], 1), F32)
    outs = []
    for j in range(mask_f.shape[1] // LANES):
        blk = mask_f[:, j * LANES:(j + 1) * LANES]
        inc = jnp.dot(blk.astype(BF16), upper, preferred_element_type=F32)
        outs.append(inc - blk + off)
        off = off + inc[:, LANES - 1:LANES]
    return jnp.concatenate(outs, axis=1)


def _topk_kernel(aff_ref, idx_ref, gate_ref, *, cap):
    aff = aff_ref[0]
    n_e, t = aff.shape

    def search(i, thr_bits):
        cand = thr_bits | (jnp.int32(1) << (29 - i))
        cnt = jnp.sum((aff >= pltpu.bitcast(cand, F32)).astype(F32), axis=1, keepdims=True)
        return jnp.where(cnt >= cap, cand, thr_bits)

    thr = pltpu.bitcast(lax.fori_loop(0, 30, search, jnp.zeros((n_e, 1), I32)), F32)
    gt = aff > thr
    eq = aff == thr
    need = cap - jnp.sum(gt.astype(F32), axis=1, keepdims=True)
    eq_f = eq.astype(F32)
    take_eq = jnp.where(_prefix_exclusive(eq_f) < need, eq_f, 0.0)
    sel_f = jnp.where(gt, 1.0, take_eq)
    slot = jnp.where(sel_f > 0.0, _prefix_exclusive(sel_f), -1.0)

    tok = lax.broadcasted_iota(I32, (1, t), 1)
    tok_hi = (tok >> TOK_RADIX_BITS).astype(F32)
    tok_lo = (tok & (TOK_RADIX - 1)).astype(F32)
    sub = lax.broadcasted_iota(I32, (8, t), 0)
    slot_b = slot.astype(BF16)
    slot_ids = lax.broadcasted_iota(I32, (cap, 1), 0).astype(F32).astype(BF16)
    one_b = jnp.ones((cap, t), BF16)
    zero_b = jnp.zeros((cap, t), BF16)
    lane = lax.broadcasted_iota(I32, (cap, LANES), 1)
    idx_acc = jnp.zeros((cap, LANES), F32)
    gate_acc = jnp.zeros((cap, LANES), F32)
    for e in range(n_e):
        g = aff[e:e + 1, :]
        g_hi = g.astype(BF16).astype(F32)
        g_mid = (g - g_hi).astype(BF16).astype(F32)
        g_lo = g - g_hi - g_mid
        cols = jnp.zeros((8, t), F32)
        for k, row in enumerate((tok_hi, tok_lo, g_hi, g_mid, g_lo)):
            cols = jnp.where(sub == k, row, cols)
        hit = jnp.where(slot_b[e:e + 1, :] == slot_ids, one_b, zero_b)
        picked = lax.dot_general(hit, cols.astype(BF16), NT_DIMS, preferred_element_type=F32)
        idx_col = picked[:, 0:1] * TOK_RADIX + picked[:, 1:2]
        gate_col = picked[:, 2:3] + picked[:, 3:4] + picked[:, 4:5]
        idx_acc = jnp.where(lane == e, idx_col, idx_acc)
        gate_acc = jnp.where(lane == e, gate_col, gate_acc)
    idx_ref[0] = idx_acc.astype(I32)
    gate_ref[0] = gate_acc


def _topk(aff_t, cap):
    b, n_e, t = aff_t.shape
    out = pl.BlockSpec((1, cap, LANES), lambda bi: (bi, 0, 0))
    return pl.pallas_call(
        functools.partial(_topk_kernel, cap=cap),
        grid=(b,),
        in_specs=[pl.BlockSpec((1, n_e, t), lambda bi: (bi, 0, 0))],
        out_specs=[out, out],
        out_shape=[jax.ShapeDtypeStruct((b, cap, LANES), I32),
                   jax.ShapeDtypeStruct((b, cap, LANES), F32)],
        compiler_params=_params("parallel"),
        name="topk",
    )(aff_t)


def _ffn_kernel(idx_ref, idx_next_ref, xn_hbm, wg_ref, wu_ref, wd_ref, y_ref,
                xs_a, xs_b, sems, fence_sem, *, cap):
    nb = pl.num_programs(1)
    step = pl.program_id(0) * nb + pl.program_id(1)
    last = pl.num_programs(0) * nb - 1
    rows = FFN_G * cap
    bufs = (xs_a, xs_b)

    def row_copy(ids_ref, batch0, g, r, half):
        src_row = pl.multiple_of(ids_ref[half * FFN_G + g, 0, 0, r], ROW_TILES)
        return pltpu.make_async_copy(xn_hbm.at[batch0 + half * FFN_G + g, pl.ds(src_row, ROW_TILES), :],
                                     bufs[half].at[pl.ds((g * cap + r) * ROW_TILES, ROW_TILES), :],
                                     sems.at[half])

    def wait_half(half):
        pltpu.make_async_copy(xn_hbm.at[0, pl.ds(0, rows * ROW_TILES), :], bufs[half], sems.at[half]).wait()

    def expert_up(half):
        xs = _load_token_major(bufs[half], (), rows)
        hg = jnp.dot(xs, wg_ref[0], preferred_element_type=F32, precision=lax.Precision.DEFAULT)
        hu = jnp.dot(xs, wu_ref[0], preferred_element_type=F32, precision=lax.Precision.DEFAULT)
        return _silu(hg) * hu

    def expert_down(half, h):
        y = jnp.dot(h, wd_ref[0], preferred_element_type=F32, precision=lax.Precision.DEFAULT)
        for g in range(FFN_G):
            _store_token_major(y_ref, (half * FFN_G + g, 0), y[g * cap:(g + 1) * cap])

    batch0 = pl.program_id(1) * (2 * FFN_G)

    @pl.when(step == 0)
    def _():
        for g in range(FFN_G):
            def issue(r, carry, g=g):
                row_copy(idx_ref, batch0, g, r, 0).start()
                return carry
            lax.fori_loop(0, cap, issue, 0)
        wait_half(0)

    for g in range(FFN_G):
        for r in range(cap):
            row_copy(idx_ref, batch0, g, r, 1).start()
    h0 = expert_up(0)
    wait_half(1)
    expert_down(0, h0)
    h1 = expert_up(1)

    nxt_batch0 = (jnp.minimum(step + 1, last) % nb) * (2 * FFN_G)
    for g in range(FFN_G):
        for r in range(cap):
            row_copy(idx_next_ref, nxt_batch0, g, r, 0).start()
    pl.semaphore_signal(fence_sem, 1)
    pl.semaphore_wait(fence_sem, 1)
    expert_down(1, h1)
    wait_half(0)


def _ffn(rows4, xn2, wg, wu, wd, cap):
    b = xn2.shape[0]
    n_e, d, _ = wg.shape
    blk = 2 * FFN_G
    nb = b // blk
    wspec = pl.BlockSpec((1, d, d), lambda e, bg: (e, 0, 0))

    def next_ids(e, bg):
        nxt = jnp.minimum(e * nb + bg + 1, n_e * nb - 1)
        return (nxt % nb, nxt // nb, 0, 0)

    return pl.pallas_call(
        functools.partial(_ffn_kernel, cap=cap),
        grid=(n_e, nb),
        in_specs=[
            pl.BlockSpec((blk, 1, 1, cap), lambda e, bg: (bg, e, 0, 0), memory_space=pltpu.SMEM),
            pl.BlockSpec((blk, 1, 1, cap), next_ids, memory_space=pltpu.SMEM),
            pl.BlockSpec(memory_space=pl.ANY),
            wspec, wspec, wspec,
        ],
        out_specs=pl.BlockSpec((blk, 1, cap * ROW_TILES, LANES), lambda e, bg: (bg, e, 0, 0)),
        out_shape=jax.ShapeDtypeStruct((b, n_e, cap * ROW_TILES, LANES), F32),
        scratch_shapes=[pltpu.VMEM((FFN_G * cap * ROW_TILES, LANES), F32),
                        pltpu.VMEM((FFN_G * cap * ROW_TILES, LANES), F32),
                        pltpu.SemaphoreType.DMA((2,)), pltpu.SemaphoreType.REGULAR],
        compiler_params=_params("arbitrary", "arbitrary"),
        name="ffn",
    )(rows4, rows4, xn2, wg, wu, wd)


def _combine_kernel(row_ref, gate_ref, y_ref, h_ref, mod_ref, g_ref, o_ref, ff_scr, *, cap):
    group = pl.program_id(1)

    @pl.when(group == 0)
    def _():
        ff_scr[...] = jnp.zeros_like(ff_scr)

    for e in range(COMBINE_EXPERTS):
        for r0 in range(0, cap, COMBINE_UNROLL):
            dst = [pl.ds(pl.multiple_of(row_ref[0, e, 0, r0 + u], ROW_TILES), ROW_TILES)
                   for u in range(COMBINE_UNROLL)]
            sums = [ff_scr[dst[u], :]
                    + y_ref[0, e, (r0 + u) * ROW_TILES:(r0 + u + 1) * ROW_TILES, :] * gate_ref[0, e, 0, r0 + u]
                    for u in range(COMBINE_UNROLL)]
            for u in range(COMBINE_UNROLL):
                ff_scr[dst[u], :] = sums[u]

    @pl.when(group == pl.num_programs(1) - 1)
    def _():
        g2 = mod_ref[0, :, 5 * D_MODEL:6 * D_MODEL]
        g = g_ref[...]
        for c in range(h_ref.shape[1] // FINAL_ROWS):
            rows = slice(c * FINAL_ROWS, (c + 1) * FINAL_ROWS)
            ff = jnp.concatenate(
                [ff_scr[pl.ds(c * FINAL_ROWS * ROW_TILES + j, FINAL_ROWS, stride=ROW_TILES), :]
                 for j in range(ROW_TILES)], axis=1)
            o_ref[0, rows, :] = h_ref[0, rows, :] + g2 * _rms(ff, g)


def _combine(rows4, gate4, y, h1, mod3, g, cap):
    b, t, d = h1.shape
    n_e = y.shape[1]
    scalars = pl.BlockSpec((1, COMBINE_EXPERTS, 1, cap), lambda bi, e: (bi, e, 0, 0), memory_space=pltpu.SMEM)
    whole = pl.BlockSpec((1, t, d), lambda bi, e: (bi, 0, 0))
    return pl.pallas_call(
        functools.partial(_combine_kernel, cap=cap),
        grid=(b, n_e // COMBINE_EXPERTS),
        in_specs=[scalars, scalars,
                  pl.BlockSpec((1, COMBINE_EXPERTS, cap * ROW_TILES, LANES), lambda bi, e: (bi, e, 0, 0)),
                  whole,
                  pl.BlockSpec((1, 1, 6 * d), lambda bi, e: (bi, 0, 0)),
                  pl.BlockSpec((1, d), lambda bi, e: (0, 0))],
        out_specs=pl.BlockSpec((1, t, d), lambda bi, e: (bi, 0, 0), pipeline_mode=pl.Buffered(1)),
        out_shape=jax.ShapeDtypeStruct((b, t, d), F32),
        scratch_shapes=[pltpu.VMEM((t * ROW_TILES, LANES), F32)],
        compiler_params=pltpu.CompilerParams(dimension_semantics=("parallel", "arbitrary"),
                                             vmem_limit_bytes=COMBINE_VMEM_LIMIT),
        name="combine",
    )(rows4, gate4, y, h1, mod3, g)


def _rope_tables(t):
    rows = t // GRID_W
    row = jnp.repeat(jnp.arange(rows, dtype=F32), GRID_W)
    col = jnp.tile(jnp.arange(GRID_W, dtype=F32), rows)
    inv = ROPE_BASE ** (-jnp.arange(0, ROT_AXIS, 2, dtype=F32) / ROT_AXIS)
    ang_r = row[:, None] * inv[None, :]
    ang_c = col[:, None] * inv[None, :]
    ang = jnp.concatenate([ang_r, ang_r, ang_c, ang_c], axis=-1)
    ang = jnp.concatenate([ang, ang], axis=-1)
    cos, sin = jnp.cos(ang), jnp.sin(ang)
    low = (jnp.arange(LANES) % ROT_AXIS) < ROT_AXIS // 2
    sin_lo = jnp.where(low[None, :], -sin, 0.0)
    sin_hi = jnp.where(low[None, :], 0.0, sin)
    scale = jnp.float32(math.log2(math.e) / math.sqrt(HEAD_DIM))
    return jnp.concatenate([cos * scale, sin_lo * scale, sin_hi * scale, cos, sin_lo, sin_hi], axis=-1)


def kernel(x, c, ctx, c_ctx, w_ada, b_ada, norm_pre_mix, norm_post_mix, norm_pre_ffn, norm_post_ffn,
           w_in, conv_w, lambda_q1, lambda_k1, lambda_q2, lambda_k2, subln_g, w_out, w_router,
           w_gate, w_up, w_down):
    b, t, d = x.shape
    depth = w_ada.shape[0]
    assert depth == 1 and d == D_MODEL and t % TM == 0 and t % TQ_BLOCK == 0 and TQ_BLOCK % TQ == 0 and b % (2 * FFN_G) == 0
    assert b + 1 <= MOD_ROWS
    layer = 0
    lam_init = 0.8 - 0.6 * math.exp(-0.3 * layer)
    cap = CAP_FACTOR * t // N_EXPERTS

    cs = jnp.concatenate([c, c_ctx[None, :], jnp.zeros((MOD_ROWS - b - 1, d), F32)], axis=0)
    mod3 = _ada(cs, w_ada[layer], b_ada[layer][None, :]).reshape(MOD_ROWS, 1, 6 * d)

    w_in_bf = w_in[layer].astype(BF16)
    q, k, v, conv = _inproj(x, mod3, norm_pre_mix[layer][None, :], w_in_bf, conv_w[layer], _rope_tables(t))
    kc, vc = _ctxproj(ctx, mod3, norm_pre_mix[layer][None, :], w_in_bf)
    attn = _attn(q, k, v, kc, vc, lambda_q1[layer][None, :], lambda_k1[layer][None, :],
                 lambda_q2[layer][None, :], lambda_k2[layer][None, :], subln_g[layer][None, :], lam_init)
    h1, xn2, aff_t = _outproj(attn, conv, x, mod3, norm_post_mix[layer][None, :],
                              norm_pre_ffn[layer][None, :], w_out[layer].astype(BF16), w_router[layer].T)

    idx_p, gate_p = _topk(aff_t, cap)
    rows4 = (jnp.swapaxes(idx_p[:, :, :N_EXPERTS], 1, 2) * ROW_TILES).reshape(b, N_EXPERTS, 1, cap)
    gate4 = jnp.swapaxes(gate_p[:, :, :N_EXPERTS], 1, 2).reshape(b, N_EXPERTS, 1, cap)

    y = _ffn(rows4, xn2, w_gate[layer], w_up[layer], w_down[layer], cap)
    return _combine(rows4, gate4, y, h1, mod3, norm_post_ffn[layer][None, :], cap)
```

```python
import functools
import math

import jax
import jax.numpy as jnp
from jax import lax
from jax.experimental import pallas as pl
from jax.experimental.pallas import tpu as pltpu

F32 = jnp.float32
BF16 = jnp.bfloat16
I32 = jnp.int32

D_MODEL = 1024
CTX_LEN = 256
GRID_W = 64
HEADS = 4
HEAD_DIM = 64
V_DIM = 2 * HEAD_DIM
QK_W = HEADS * 2 * HEAD_DIM
V_W = HEADS * V_DIM
CONV_W = 512
PROJ_W = 2 * QK_W + V_W + 3 * CONV_W
ROPE_BASE = 10000.0
ROT_AXIS = HEAD_DIM // 2
N_EXPERTS = 16
CAP_FACTOR = 2
EPS = 1e-6
LANES = 128
ROW_TILES = D_MODEL // LANES
MOD_ROWS = 40

TM = 1024
TQ = 256
TQ_BLOCK = 512
FFN_G = 2
TOK_RADIX_BITS = 6
TOK_RADIX = 1 << TOK_RADIX_BITS
COMBINE_UNROLL = 8
COMBINE_EXPERTS = 4
FINAL_ROWS = 256
COMBINE_VMEM_LIMIT = 56 * 1024 * 1024
VMEM_LIMIT = 48 * 1024 * 1024
HIGHEST = lax.Precision.HIGHEST
NT_DIMS = (((1,), (1,)), ((), ()))


def _params(*sem):
    return pltpu.CompilerParams(dimension_semantics=sem, vmem_limit_bytes=VMEM_LIMIT)


def _rms(x, g):
    return x * lax.rsqrt(jnp.mean(x * x, axis=-1, keepdims=True) + EPS) * g


def _silu(x):
    return x * jax.nn.sigmoid(x)


def _store_token_major(ref, lead, x):
    for j in range(ROW_TILES):
        ref[lead + (pl.ds(j, x.shape[0], stride=ROW_TILES), slice(None))] = x[:, j * LANES:(j + 1) * LANES]


def _load_token_major(ref, lead, rows):
    return jnp.concatenate([ref[lead + (pl.ds(j, rows, stride=ROW_TILES), slice(None))]
                            for j in range(ROW_TILES)], axis=1)


def _ada_kernel(cs_ref, w_ref, b_ref, o_ref):
    s = _silu(cs_ref[...])
    o_ref[...] = jnp.dot(s, w_ref[...], preferred_element_type=F32, precision=HIGHEST) + b_ref[...]


def _ada(cs, w, b):
    n = w.shape[1]
    tn = 512
    return pl.pallas_call(
        _ada_kernel,
        grid=(n // tn,),
        in_specs=[pl.BlockSpec((MOD_ROWS, D_MODEL), lambda j: (0, 0)),
                  pl.BlockSpec((D_MODEL, tn), lambda j: (0, j)),
                  pl.BlockSpec((1, tn), lambda j: (0, j))],
        out_specs=pl.BlockSpec((MOD_ROWS, tn), lambda j: (0, j)),
        out_shape=jax.ShapeDtypeStruct((MOD_ROWS, n), F32),
        compiler_params=_params("parallel"),
        name="ada",
    )(cs, w, b)


def _rope_store(p, tabs, out_ref):
    cos, sin_lo, sin_hi = tabs
    for j in range(QK_W // LANES):
        xj = p[:, j * LANES:(j + 1) * LANES]
        r = xj * cos + pltpu.roll(xj, LANES - 16, 1) * sin_lo + pltpu.roll(xj, 16, 1) * sin_hi
        out_ref[0, :, j * LANES:(j + 1) * LANES] = r.astype(BF16)


def _store_values(v_ref, pv):
    ones = jnp.ones((pv.shape[0], V_DIM), BF16)
    for h in range(HEADS):
        v_ref[0, :, 2 * h * V_DIM:(2 * h + 1) * V_DIM] = pv[:, h * V_DIM:(h + 1) * V_DIM].astype(BF16)
        v_ref[0, :, (2 * h + 1) * V_DIM:(2 * h + 2) * V_DIM] = ones


def _in_kernel(x_ref, xp_ref, xn_ref, mod_ref, g_ref, w_ref, cw_ref, tab_ref,
               q_ref, k_ref, v_ref, c_ref):
    i = pl.program_id(1)
    last = pl.num_programs(1) - 1
    g = g_ref[...]
    sh = mod_ref[0, :, 0:D_MODEL]
    sc = mod_ref[0, :, D_MODEL:2 * D_MODEL]

    def norm_mod(xt):
        return (_rms(xt, g) * (1.0 + sc) + sh).astype(BF16)

    def proj(xb, c0, width):
        return jnp.dot(xb, w_ref[:, c0:c0 + width], preferred_element_type=F32)

    xb = norm_mod(x_ref[0])
    tabs = [tab_ref[:, j * LANES:(j + 1) * LANES] for j in range(6)]
    _rope_store(proj(xb, 0, QK_W), tabs[0:3], q_ref)
    _rope_store(proj(xb, QK_W, QK_W), tabs[3:6], k_ref)
    _store_values(v_ref, proj(xb, 2 * QK_W, V_W))

    c0 = 2 * QK_W + V_W
    gb = proj(xb, c0, CONV_W)
    u = proj(xb, c0 + CONV_W, CONV_W) * proj(xb, c0 + 2 * CONV_W, CONV_W)
    xh = norm_mod(jnp.concatenate([xp_ref[0], xn_ref[0]], axis=0))
    ph = proj(xh, c0 + CONV_W, 2 * CONV_W)
    uh = ph[:, :CONV_W] * ph[:, CONV_W:]
    u_before = jnp.where(i > 0, uh[7:8], 0.0)
    u_after = jnp.where(i < last, uh[8:9], 0.0)
    rows = lax.broadcasted_iota(I32, u.shape, 0)
    u_prev = jnp.where(rows == 0, u_before, pltpu.roll(u, 1, 0))
    u_next = jnp.where(rows == TM - 1, u_after, pltpu.roll(u, TM - 1, 0))
    cw = cw_ref[...]
    y = cw[0:1] * u_prev + cw[1:2] * u + cw[2:3] * u_next
    c_ref[0] = (gb * y).astype(BF16)


def _inproj(x, mod3, g, w_bf, conv_w, tab):
    b, t, _ = x.shape
    nt = t // TM
    hb = TM // 8
    out = jax.ShapeDtypeStruct((b, t, QK_W), BF16)
    tile = pl.BlockSpec((1, TM, QK_W), lambda bi, i: (bi, i, 0))
    return pl.pallas_call(
        _in_kernel,
        grid=(b, nt),
        in_specs=[
            pl.BlockSpec((1, TM, D_MODEL), lambda bi, i: (bi, i, 0)),
            pl.BlockSpec((1, 8, D_MODEL), lambda bi, i: (bi, jnp.maximum(i * hb - 1, 0), 0)),
            pl.BlockSpec((1, 8, D_MODEL), lambda bi, i: (bi, jnp.minimum((i + 1) * hb, t // 8 - 1), 0)),
            pl.BlockSpec((1, 1, 6 * D_MODEL), lambda bi, i: (bi, 0, 0)),
            pl.BlockSpec((1, D_MODEL), lambda bi, i: (0, 0)),
            pl.BlockSpec((D_MODEL, PROJ_W), lambda bi, i: (0, 0)),
            pl.BlockSpec((3, CONV_W), lambda bi, i: (0, 0)),
            pl.BlockSpec((TM, 6 * LANES), lambda bi, i: (i, 0)),
        ],
        out_specs=[tile, tile, pl.BlockSpec((1, TM, 2 * V_W), lambda bi, i: (bi, i, 0)), tile],
        out_shape=[out, out, jax.ShapeDtypeStruct((b, t, 2 * V_W), BF16), out],
        compiler_params=_params("parallel", "parallel"),
        name="inproj",
    )(x, x, x, mod3, g, w_bf, conv_w, tab)


def _ctx_kernel(x_ref, mod_ref, g_ref, wk_ref, wv_ref, k_ref, v_ref):
    sh = mod_ref[0, :, 0:D_MODEL]
    sc = mod_ref[0, :, D_MODEL:2 * D_MODEL]
    xb = (_rms(x_ref[0], g_ref[...]) * (1.0 + sc) + sh).astype(BF16)
    k_ref[0] = jnp.dot(xb, wk_ref[...], preferred_element_type=F32).astype(BF16)
    _store_values(v_ref, jnp.dot(xb, wv_ref[...], preferred_element_type=F32))


def _ctxproj(ctx, mod3, g, w_bf):
    b = ctx.shape[0]
    out = jax.ShapeDtypeStruct((b, CTX_LEN, QK_W), BF16)
    tile = pl.BlockSpec((1, CTX_LEN, QK_W), lambda bi: (bi, 0, 0))
    return pl.pallas_call(
        _ctx_kernel,
        grid=(b,),
        in_specs=[
            pl.BlockSpec((1, CTX_LEN, D_MODEL), lambda bi: (bi, 0, 0)),
            pl.BlockSpec((1, 1, 6 * D_MODEL), lambda bi: (b, 0, 0)),
            pl.BlockSpec((1, D_MODEL), lambda bi: (0, 0)),
            pl.BlockSpec((D_MODEL, QK_W), lambda bi: (0, 1)),
            pl.BlockSpec((D_MODEL, V_W), lambda bi: (0, 2)),
        ],
        out_specs=[tile, pl.BlockSpec((1, CTX_LEN, 2 * V_W), lambda bi: (bi, 0, 0))],
        out_shape=[out, jax.ShapeDtypeStruct((b, CTX_LEN, 2 * V_W), BF16)],
        compiler_params=_params("parallel"),
        name="ctxproj",
    )(ctx, mod3, g, w_bf, w_bf)


def _attn_kernel(q_ref, k_ref, v_ref, kc_ref, vc_ref, lq1_ref, lk1_ref, lq2_ref, lk2_ref,
                 sg_ref, o_ref, *, lam_init):
    lam = (jnp.exp(jnp.sum(lq1_ref[...] * lk1_ref[...], keepdims=True))
           - jnp.exp(jnp.sum(lq2_ref[...] * lk2_ref[...], keepdims=True)) + lam_init)
    lane = lax.broadcasted_iota(I32, (1, V_DIM), 1)
    first = (lane < HEAD_DIM).astype(BF16)
    second = (lane >= HEAD_DIM).astype(BF16)
    sg = sg_ref[...] * (1.0 - lam_init)

    for h, u in [(h, u) for u in range(TQ_BLOCK // TQ) for h in range(HEADS)]:
        sl = slice(h * V_DIM, (h + 1) * V_DIM)
        sv = slice(h * 2 * V_DIM, (h + 1) * 2 * V_DIM)
        rows = slice(u * TQ, (u + 1) * TQ)
        qh = q_ref[0, rows, sl]
        q2 = jnp.concatenate([qh * first, qh * second], axis=0)
        s_l = lax.dot_general(q2, k_ref[0, :, sl], NT_DIMS, preferred_element_type=F32)
        s_c = lax.dot_general(q2, kc_ref[0, :, sl], NT_DIMS, preferred_element_type=F32)
        m = jnp.maximum(jnp.max(s_l, axis=1, keepdims=True), jnp.max(s_c, axis=1, keepdims=True))
        p_l = jnp.exp2(s_l - m).astype(BF16)
        p_c = jnp.exp2(s_c - m).astype(BF16)
        pv = (jnp.dot(p_l, v_ref[0, :, sv], preferred_element_type=F32)
              + jnp.dot(p_c, vc_ref[0, :, sv], preferred_element_type=F32))
        num = pv[:, :V_DIM]
        den = pv[:, V_DIM:V_DIM + 1]
        o = num[:TQ] * (1.0 / den[:TQ]) - num[TQ:] * (lam / den[TQ:])
        o_ref[0, rows, sl] = _rms(o, sg).astype(BF16)


def _attn(q, k, v, kc, vc, lq1, lk1, lq2, lk2, sg, lam_init):
    b, t, _ = q.shape
    qt = pl.BlockSpec((1, TQ_BLOCK, QK_W), lambda bi, i: (bi, i, 0))
    vec = pl.BlockSpec((1, HEAD_DIM), lambda bi, i: (0, 0))

    def per_batch(rows, width):
        return pl.BlockSpec((1, rows, width), lambda bi, i: (bi, 0, 0))

    return pl.pallas_call(
        functools.partial(_attn_kernel, lam_init=lam_init),
        grid=(b, t // TQ_BLOCK),
        in_specs=[qt, per_batch(t, QK_W), per_batch(t, 2 * V_W),
                  per_batch(CTX_LEN, QK_W), per_batch(CTX_LEN, 2 * V_W),
                  vec, vec, vec, vec, pl.BlockSpec((1, V_DIM), lambda bi, i: (0, 0))],
        out_specs=qt,
        out_shape=jax.ShapeDtypeStruct((b, t, V_W), BF16),
        compiler_params=_params("parallel", "parallel"),
        name="attn",
    )(q, k, v, kc, vc, lq1, lk1, lq2, lk2, sg)


def _out_kernel(a_ref, c_ref, x_ref, mod_ref, gpost_ref, gpre_ref, wo_ref, wr_ref,
                h_ref, xn_ref, aff_ref):
    mix = (jnp.dot(a_ref[0], wo_ref[0:V_W, :], preferred_element_type=F32)
           + jnp.dot(c_ref[0], wo_ref[V_W:V_W + CONV_W, :], preferred_element_type=F32))
    g1 = mod_ref[0, :, 2 * D_MODEL:3 * D_MODEL]
    sh2 = mod_ref[0, :, 3 * D_MODEL:4 * D_MODEL]
    sc2 = mod_ref[0, :, 4 * D_MODEL:5 * D_MODEL]
    h1 = x_ref[0] + g1 * _rms(mix, gpost_ref[...])
    h_ref[0] = h1
    xn2 = _rms(h1, gpre_ref[...]) * (1.0 + sc2) + sh2
    _store_token_major(xn_ref, (0,), xn2)
    def split(a):
        hi = a.astype(BF16)
        return hi, (a - hi.astype(F32)).astype(BF16)

    w_hi, w_lo = split(wr_ref[...])
    x_hi, x_lo = split(xn2)
    both = lax.dot_general(jnp.concatenate([w_hi, w_lo], axis=0), x_hi, NT_DIMS,
                           preferred_element_type=F32)
    logits = (both[:N_EXPERTS] + both[N_EXPERTS:]
              + lax.dot_general(w_hi, x_lo, NT_DIMS, preferred_element_type=F32))
    e = jnp.exp(logits - jnp.max(logits, axis=0, keepdims=True))
    aff_ref[0] = e / jnp.sum(e, axis=0, keepdims=True)


def _outproj(attn, conv, x, mod3, gpost, gpre, wo_bf, wr_t):
    b, t, _ = x.shape
    half = pl.BlockSpec((1, TM, V_W), lambda bi, i: (bi, i, 0))
    tile = pl.BlockSpec((1, TM, D_MODEL), lambda bi, i: (bi, i, 0))
    row = pl.BlockSpec((1, D_MODEL), lambda bi, i: (0, 0))
    act = jax.ShapeDtypeStruct((b, t, D_MODEL), F32)
    return pl.pallas_call(
        _out_kernel,
        grid=(b, t // TM),
        in_specs=[half, half, tile,
                  pl.BlockSpec((1, 1, 6 * D_MODEL), lambda bi, i: (bi, 0, 0)),
                  row, row,
                  pl.BlockSpec((V_W + CONV_W, D_MODEL), lambda bi, i: (0, 0)),
                  pl.BlockSpec((N_EXPERTS, D_MODEL), lambda bi, i: (0, 0))],
        out_specs=[tile, pl.BlockSpec((1, TM * ROW_TILES, LANES), lambda bi, i: (bi, i, 0)),
                   pl.BlockSpec((1, N_EXPERTS, TM), lambda bi, i: (bi, 0, i))],
        out_shape=[act, jax.ShapeDtypeStruct((b, t * ROW_TILES, LANES), F32),
                   jax.ShapeDtypeStruct((b, N_EXPERTS, t), F32)],
        compiler_params=_params("parallel", "parallel"),
        name="outproj",
    )(attn, conv, x, mod3, gpost, gpre, wo_bf, wr_t)


def _prefix_exclusive(mask_f):
    r = lax.broadcasted_iota(I32, (LANES, LANES), 0)
    c = lax.broadcasted_iota(I32, (LANES, LANES), 1)
    upper = (r <= c).astype(BF16)
    off = jnp.zeros((mask_f.shape[0], 1), F32)
    outs = []
    for j in range(mask_f.shape[1] // LANES):
        blk = mask_f[:, j * LANES:(j + 1) * LANES]
        inc = jnp.dot(blk.astype(BF16), upper, preferred_element_type=F32)
        outs.append(inc - blk + off)
        off = off + inc[:, LANES - 1:LANES]
    return jnp.concatenate(outs, axis=1)


def _topk_kernel(aff_ref, idx_ref, gate_ref, *, cap):
    aff = aff_ref[0]
    n_e, t = aff.shape

    def search(i, thr_bits):
        cand = thr_bits | (jnp.int32(1) << (29 - i))
        cnt = jnp.sum((aff >= pltpu.bitcast(cand, F32)).astype(F32), axis=1, keepdims=True)
        return jnp.where(cnt >= cap, cand, thr_bits)

    thr = pltpu.bitcast(lax.fori_loop(0, 30, search, jnp.zeros((n_e, 1), I32)), F32)
    gt = aff > thr
    eq = aff == thr
    need = cap - jnp.sum(gt.astype(F32), axis=1, keepdims=True)
    eq_f = eq.astype(F32)
    take_eq = jnp.where(_prefix_exclusive(eq_f) < need, eq_f, 0.0)
    sel_f = jnp.where(gt, 1.0, take_eq)
    slot = jnp.where(sel_f > 0.0, _prefix_exclusive(sel_f), -1.0)

    tok = lax.broadcasted_iota(I32, (1, t), 1)
    tok_hi = (tok >> TOK_RADIX_BITS).astype(F32)
    tok_lo = (tok & (TOK_RADIX - 1)).astype(F32)
    sub = lax.broadcasted_iota(I32, (8, t), 0)
    slot_b = slot.astype(BF16)
    slot_ids = lax.broadcasted_iota(I32, (cap, 1), 0).astype(F32).astype(BF16)
    one_b = jnp.ones((cap, t), BF16)
    zero_b = jnp.zeros((cap, t), BF16)
    lane = lax.broadcasted_iota(I32, (cap, LANES), 1)
    idx_acc = jnp.zeros((cap, LANES), F32)
    gate_acc = jnp.zeros((cap, LANES), F32)
    for e in range(n_e):
        g = aff[e:e + 1, :]
        g_hi = g.astype(BF16).astype(F32)
        g_mid = (g - g_hi).astype(BF16).astype(F32)
        g_lo = g - g_hi - g_mid
        cols = jnp.zeros((8, t), F32)
        for k, row in enumerate((tok_hi, tok_lo, g_hi, g_mid, g_lo)):
            cols = jnp.where(sub == k, row, cols)
        hit = jnp.where(slot_b[e:e + 1, :] == slot_ids, one_b, zero_b)
        picked = lax.dot_general(hit, cols.astype(BF16), NT_DIMS, preferred_element_type=F32)
        idx_col = picked[:, 0:1] * TOK_RADIX + picked[:, 1:2]
        gate_col = picked[:, 2:3] + picked[:, 3:4] + picked[:, 4:5]
        idx_acc = jnp.where(lane == e, idx_col, idx_acc)
        gate_acc = jnp.where(lane == e, gate_col, gate_acc)
    idx_ref[0] = idx_acc.astype(I32)
    gate_ref[0] = gate_acc


def _topk(aff_t, cap):
    b, n_e, t = aff_t.shape
    out = pl.BlockSpec((1, cap, LANES), lambda bi: (bi, 0, 0))
    return pl.pallas_call(
        functools.partial(_topk_kernel, cap=cap),
        grid=(b,),
        in_specs=[pl.BlockSpec((1, n_e, t), lambda bi: (bi, 0, 0))],
        out_specs=[out, out],
        out_shape=[jax.ShapeDtypeStruct((b, cap, LANES), I32),
                   jax.ShapeDtypeStruct((b, cap, LANES), F32)],
        compiler_params=_params("parallel"),
        name="topk",
    )(aff_t)


def _ffn_kernel(idx_ref, idx_next_ref, xn_hbm, wg_ref, wu_ref, wd_ref, y_ref, xs_ref, sems, *, cap):
    nb = pl.num_programs(1)
    step = pl.program_id(0) * nb + pl.program_id(1)
    last = pl.num_programs(0) * nb - 1
    rows = FFN_G * cap
    slot = step % 2

    def row_copy(ids_ref, batch0, k, r, dst_slot):
        src_row = pl.multiple_of(ids_ref[k, 0, 0, r], ROW_TILES)
        return pltpu.make_async_copy(xn_hbm.at[batch0 + k, pl.ds(src_row, ROW_TILES), :],
                                     xs_ref.at[dst_slot, pl.ds((k * cap + r) * ROW_TILES, ROW_TILES), :],
                                     sems.at[dst_slot])

    def wait_slot(s):
        pltpu.make_async_copy(xn_hbm.at[0, pl.ds(0, 2 * rows * ROW_TILES), :], xs_ref.at[s], sems.at[s]).wait()

    def expert(half):
        xs = jnp.concatenate(
            [xs_ref[slot, pl.ds(half * rows * ROW_TILES + j, rows, stride=ROW_TILES), :]
             for j in range(ROW_TILES)], axis=1)
        hg = jnp.dot(xs, wg_ref[0], preferred_element_type=F32, precision=lax.Precision.DEFAULT)
        hu = jnp.dot(xs, wu_ref[0], preferred_element_type=F32, precision=lax.Precision.DEFAULT)
        y = jnp.dot(_silu(hg) * hu, wd_ref[0], preferred_element_type=F32, precision=lax.Precision.DEFAULT)
        for g in range(FFN_G):
            _store_token_major(y_ref, (half * FFN_G + g, 0), y[g * cap:(g + 1) * cap])

    @pl.when(step == 0)
    def _():
        for k in range(2 * FFN_G):
            def issue(r, carry, k=k):
                row_copy(idx_ref, 0, k, r, 0).start()
                return carry
            lax.fori_loop(0, cap, issue, 0)

    wait_slot(slot)
    expert(0)
    nxt_batch0 = (jnp.minimum(step + 1, last) % nb) * (2 * FFN_G)
    for k in range(2 * FFN_G):
        for r in range(cap):
            row_copy(idx_next_ref, nxt_batch0, k, r, 1 - slot).start()
    expert(1)

    @pl.when(step == last)
    def _():
        wait_slot(1 - slot)


def _ffn(rows4, xn2, wg, wu, wd, cap):
    b = xn2.shape[0]
    n_e, d, _ = wg.shape
    blk = 2 * FFN_G
    nb = b // blk
    wspec = pl.BlockSpec((1, d, d), lambda e, bg: (e, 0, 0))

    def next_ids(e, bg):
        nxt = jnp.minimum(e * nb + bg + 1, n_e * nb - 1)
        return (nxt % nb, nxt // nb, 0, 0)

    return pl.pallas_call(
        functools.partial(_ffn_kernel, cap=cap),
        grid=(n_e, nb),
        in_specs=[
            pl.BlockSpec((blk, 1, 1, cap), lambda e, bg: (bg, e, 0, 0), memory_space=pltpu.SMEM),
            pl.BlockSpec((blk, 1, 1, cap), next_ids, memory_space=pltpu.SMEM),
            pl.BlockSpec(memory_space=pl.ANY),
            wspec, wspec, wspec,
        ],
        out_specs=pl.BlockSpec((blk, 1, cap * ROW_TILES, LANES), lambda e, bg: (bg, e, 0, 0)),
        out_shape=jax.ShapeDtypeStruct((b, n_e, cap * ROW_TILES, LANES), F32),
        scratch_shapes=[pltpu.VMEM((2, 2 * FFN_G * cap * ROW_TILES, LANES), F32),
                        pltpu.SemaphoreType.DMA((2,))],
        compiler_params=_params("arbitrary", "arbitrary"),
        name="ffn",
    )(rows4, rows4, xn2, wg, wu, wd)


def _combine_kernel(row_ref, gate_ref, y_ref, h_ref, mod_ref, g_ref, o_ref, ff_scr, *, cap):
    group = pl.program_id(1)

    @pl.when(group == 0)
    def _():
        ff_scr[...] = jnp.zeros_like(ff_scr)

    for e in range(COMBINE_EXPERTS):
        for r0 in range(0, cap, COMBINE_UNROLL):
            dst = [pl.ds(pl.multiple_of(row_ref[0, e, 0, r0 + u], ROW_TILES), ROW_TILES)
                   for u in range(COMBINE_UNROLL)]
            sums = [ff_scr[dst[u], :]
                    + y_ref[0, e, (r0 + u) * ROW_TILES:(r0 + u + 1) * ROW_TILES, :] * gate_ref[0, e, 0, r0 + u]
                    for u in range(COMBINE_UNROLL)]
            for u in range(COMBINE_UNROLL):
                ff_scr[dst[u], :] = sums[u]

    @pl.when(group == pl.num_programs(1) - 1)
    def _():
        g2 = mod_ref[0, :, 5 * D_MODEL:6 * D_MODEL]
        g = g_ref[...]
        for c in range(h_ref.shape[1] // FINAL_ROWS):
            rows = slice(c * FINAL_ROWS, (c + 1) * FINAL_ROWS)
            ff = jnp.concatenate(
                [ff_scr[pl.ds(c * FINAL_ROWS * ROW_TILES + j, FINAL_ROWS, stride=ROW_TILES), :]
                 for j in range(ROW_TILES)], axis=1)
            o_ref[0, rows, :] = h_ref[0, rows, :] + g2 * _rms(ff, g)


def _combine(rows4, gate4, y, h1, mod3, g, cap):
    b, t, d = h1.shape
    n_e = y.shape[1]
    scalars = pl.BlockSpec((1, COMBINE_EXPERTS, 1, cap), lambda bi, e: (bi, e, 0, 0), memory_space=pltpu.SMEM)
    whole = pl.BlockSpec((1, t, d), lambda bi, e: (bi, 0, 0))
    return pl.pallas_call(
        functools.partial(_combine_kernel, cap=cap),
        grid=(b, n_e // COMBINE_EXPERTS),
        in_specs=[scalars, scalars,
                  pl.BlockSpec((1, COMBINE_EXPERTS, cap * ROW_TILES, LANES), lambda bi, e: (bi, e, 0, 0)),
                  whole,
                  pl.BlockSpec((1, 1, 6 * d), lambda bi, e: (bi, 0, 0)),
                  pl.BlockSpec((1, d), lambda bi, e: (0, 0))],
        out_specs=pl.BlockSpec((1, t, d), lambda bi, e: (bi, 0, 0), pipeline_mode=pl.Buffered(1)),
        out_shape=jax.ShapeDtypeStruct((b, t, d), F32),
        scratch_shapes=[pltpu.VMEM((t * ROW_TILES, LANES), F32)],
        compiler_params=pltpu.CompilerParams(dimension_semantics=("parallel", "arbitrary"),
                                             vmem_limit_bytes=COMBINE_VMEM_LIMIT),
        name="combine",
    )(rows4, gate4, y, h1, mod3, g)


def _rope_tables(t):
    rows = t // GRID_W
    row = jnp.repeat(jnp.arange(rows, dtype=F32), GRID_W)
    col = jnp.tile(jnp.arange(GRID_W, dtype=F32), rows)
    inv = ROPE_BASE ** (-jnp.arange(0, ROT_AXIS, 2, dtype=F32) / ROT_AXIS)
    ang_r = row[:, None] * inv[None, :]
    ang_c = col[:, None] * inv[None, :]
    ang = jnp.concatenate([ang_r, ang_r, ang_c, ang_c], axis=-1)
    ang = jnp.concatenate([ang, ang], axis=-1)
    cos, sin = jnp.cos(ang), jnp.sin(ang)
    low = (jnp.arange(LANES) % ROT_AXIS) < ROT_AXIS // 2
    sin_lo = jnp.where(low[None, :], -sin, 0.0)
    sin_hi = jnp.where(low[None, :], 0.0, sin)
    scale = jnp.float32(math.log2(math.e) / math.sqrt(HEAD_DIM))
    return jnp.concatenate([cos * scale, sin_lo * scale, sin_hi * scale, cos, sin_lo, sin_hi], axis=-1)


def kernel(x, c, ctx, c_ctx, w_ada, b_ada, norm_pre_mix, norm_post_mix, norm_pre_ffn, norm_post_ffn,
           w_in, conv_w, lambda_q1, lambda_k1, lambda_q2, lambda_k2, subln_g, w_out, w_router,
           w_gate, w_up, w_down):
    b, t, d = x.shape
    depth = w_ada.shape[0]
    assert depth == 1 and d == D_MODEL and t % TM == 0 and t % TQ_BLOCK == 0 and TQ_BLOCK % TQ == 0 and b % (2 * FFN_G) == 0
    assert b + 1 <= MOD_ROWS
    layer = 0
    lam_init = 0.8 - 0.6 * math.exp(-0.3 * layer)
    cap = CAP_FACTOR * t // N_EXPERTS

    cs = jnp.concatenate([c, c_ctx[None, :], jnp.zeros((MOD_ROWS - b - 1, d), F32)], axis=0)
    mod3 = _ada(cs, w_ada[layer], b_ada[layer][None, :]).reshape(MOD_ROWS, 1, 6 * d)

    w_in_bf = w_in[layer].astype(BF16)
    q, k, v, conv = _inproj(x, mod3, norm_pre_mix[layer][None, :], w_in_bf, conv_w[layer], _rope_tables(t))
    kc, vc = _ctxproj(ctx, mod3, norm_pre_mix[layer][None, :], w_in_bf)
    attn = _attn(q, k, v, kc, vc, lambda_q1[layer][None, :], lambda_k1[layer][None, :],
                 lambda_q2[layer][None, :], lambda_k2[layer][None, :], subln_g[layer][None, :], lam_init)
    h1, xn2, aff_t = _outproj(attn, conv, x, mod3, norm_post_mix[layer][None, :],
                              norm_pre_ffn[layer][None, :], w_out[layer].astype(BF16), w_router[layer].T)

    idx_p, gate_p = _topk(aff_t, cap)
    rows4 = (jnp.swapaxes(idx_p[:, :, :N_EXPERTS], 1, 2) * ROW_TILES).reshape(b, N_EXPERTS, 1, cap)
    gate4 = jnp.swapaxes(gate_p[:, :, :N_EXPERTS], 1, 2).reshape(b, N_EXPERTS, 1, cap)

    y = _ffn(rows4, xn2, w_gate[layer], w_up[layer], w_down[layer], cap)
    return _combine(rows4, gate4, y, h1, mod3, norm_post_ffn[layer][None, :], cap)
```

```python
import functools
import math

import jax
import jax.numpy as jnp
from jax import lax
from jax.experimental import pallas as pl
from jax.experimental.pallas import tpu as pltpu

F32 = jnp.float32
BF16 = jnp.bfloat16
I32 = jnp.int32

D_MODEL = 1024
CTX_LEN = 256
GRID_W = 64
HEADS = 4
HEAD_DIM = 64
V_DIM = 2 * HEAD_DIM
QK_W = HEADS * 2 * HEAD_DIM
V_W = HEADS * V_DIM
CONV_W = 512
PROJ_W = 2 * QK_W + V_W + 3 * CONV_W
ROPE_BASE = 10000.0
ROT_AXIS = HEAD_DIM // 2
N_EXPERTS = 16
CAP_FACTOR = 2
EPS = 1e-6
LANES = 128
ROW_TILES = D_MODEL // LANES
MOD_ROWS = 40

TM = 1024
TQ = 256
TQ_BLOCK = 1024
FFN_G = 2
TOK_RADIX_BITS = 6
TOK_RADIX = 1 << TOK_RADIX_BITS
COMBINE_UNROLL = 8
COMBINE_EXPERTS = 4
FINAL_ROWS = 256
COMBINE_VMEM_LIMIT = 56 * 1024 * 1024
VMEM_LIMIT = 48 * 1024 * 1024
HIGHEST = lax.Precision.HIGHEST
NT_DIMS = (((1,), (1,)), ((), ()))


def _params(*sem):
    return pltpu.CompilerParams(dimension_semantics=sem, vmem_limit_bytes=VMEM_LIMIT)


def _rms(x, g):
    return x * lax.rsqrt(jnp.mean(x * x, axis=-1, keepdims=True) + EPS) * g


def _silu(x):
    return x * jax.nn.sigmoid(x)


def _store_token_major(ref, lead, x):
    for j in range(ROW_TILES):
        ref[lead + (pl.ds(j, x.shape[0], stride=ROW_TILES), slice(None))] = x[:, j * LANES:(j + 1) * LANES]


def _load_token_major(ref, lead, rows):
    return jnp.concatenate([ref[lead + (pl.ds(j, rows, stride=ROW_TILES), slice(None))]
                            for j in range(ROW_TILES)], axis=1)


def _ada_kernel(cs_ref, w_ref, b_ref, o_ref):
    s = _silu(cs_ref[...])
    o_ref[...] = jnp.dot(s, w_ref[...], preferred_element_type=F32, precision=HIGHEST) + b_ref[...]


def _ada(cs, w, b):
    n = w.shape[1]
    tn = 512
    return pl.pallas_call(
        _ada_kernel,
        grid=(n // tn,),
        in_specs=[pl.BlockSpec((MOD_ROWS, D_MODEL), lambda j: (0, 0)),
                  pl.BlockSpec((D_MODEL, tn), lambda j: (0, j)),
                  pl.BlockSpec((1, tn), lambda j: (0, j))],
        out_specs=pl.BlockSpec((MOD_ROWS, tn), lambda j: (0, j)),
        out_shape=jax.ShapeDtypeStruct((MOD_ROWS, n), F32),
        compiler_params=_params("parallel"),
        name="ada",
    )(cs, w, b)


def _rope_store(p, tabs, out_ref):
    cos, sin_lo, sin_hi = tabs
    for j in range(QK_W // LANES):
        xj = p[:, j * LANES:(j + 1) * LANES]
        r = xj * cos + pltpu.roll(xj, LANES - 16, 1) * sin_lo + pltpu.roll(xj, 16, 1) * sin_hi
        out_ref[0, :, j * LANES:(j + 1) * LANES] = r.astype(BF16)


def _store_values(v_ref, pv):
    ones = jnp.ones((pv.shape[0], V_DIM), BF16)
    for h in range(HEADS):
        v_ref[0, :, 2 * h * V_DIM:(2 * h + 1) * V_DIM] = pv[:, h * V_DIM:(h + 1) * V_DIM].astype(BF16)
        v_ref[0, :, (2 * h + 1) * V_DIM:(2 * h + 2) * V_DIM] = ones


def _in_kernel(x_ref, xp_ref, xn_ref, mod_ref, g_ref, w_ref, cw_ref, tab_ref,
               q_ref, k_ref, v_ref, c_ref):
    i = pl.program_id(1)
    last = pl.num_programs(1) - 1
    g = g_ref[...]
    sh = mod_ref[0, :, 0:D_MODEL]
    sc = mod_ref[0, :, D_MODEL:2 * D_MODEL]

    def norm_mod(xt):
        return (_rms(xt, g) * (1.0 + sc) + sh).astype(BF16)

    def proj(xb, c0, width):
        return jnp.dot(xb, w_ref[:, c0:c0 + width], preferred_element_type=F32)

    xb = norm_mod(x_ref[0])
    tabs = [tab_ref[:, j * LANES:(j + 1) * LANES] for j in range(6)]
    _rope_store(proj(xb, 0, QK_W), tabs[0:3], q_ref)
    _rope_store(proj(xb, QK_W, QK_W), tabs[3:6], k_ref)
    _store_values(v_ref, proj(xb, 2 * QK_W, V_W))

    c0 = 2 * QK_W + V_W
    gb = proj(xb, c0, CONV_W)
    u = proj(xb, c0 + CONV_W, CONV_W) * proj(xb, c0 + 2 * CONV_W, CONV_W)
    xh = norm_mod(jnp.concatenate([xp_ref[0], xn_ref[0]], axis=0))
    ph = proj(xh, c0 + CONV_W, 2 * CONV_W)
    uh = ph[:, :CONV_W] * ph[:, CONV_W:]
    u_before = jnp.where(i > 0, uh[7:8], 0.0)
    u_after = jnp.where(i < last, uh[8:9], 0.0)
    rows = lax.broadcasted_iota(I32, u.shape, 0)
    u_prev = jnp.where(rows == 0, u_before, pltpu.roll(u, 1, 0))
    u_next = jnp.where(rows == TM - 1, u_after, pltpu.roll(u, TM - 1, 0))
    cw = cw_ref[...]
    y = cw[0:1] * u_prev + cw[1:2] * u + cw[2:3] * u_next
    c_ref[0] = (gb * y).astype(BF16)


def _inproj(x, mod3, g, w_bf, conv_w, tab):
    b, t, _ = x.shape
    nt = t // TM
    hb = TM // 8
    out = jax.ShapeDtypeStruct((b, t, QK_W), BF16)
    tile = pl.BlockSpec((1, TM, QK_W), lambda bi, i: (bi, i, 0))
    return pl.pallas_call(
        _in_kernel,
        grid=(b, nt),
        in_specs=[
            pl.BlockSpec((1, TM, D_MODEL), lambda bi, i: (bi, i, 0)),
            pl.BlockSpec((1, 8, D_MODEL), lambda bi, i: (bi, jnp.maximum(i * hb - 1, 0), 0)),
            pl.BlockSpec((1, 8, D_MODEL), lambda bi, i: (bi, jnp.minimum((i + 1) * hb, t // 8 - 1), 0)),
            pl.BlockSpec((1, 1, 6 * D_MODEL), lambda bi, i: (bi, 0, 0)),
            pl.BlockSpec((1, D_MODEL), lambda bi, i: (0, 0)),
            pl.BlockSpec((D_MODEL, PROJ_W), lambda bi, i: (0, 0)),
            pl.BlockSpec((3, CONV_W), lambda bi, i: (0, 0)),
            pl.BlockSpec((TM, 6 * LANES), lambda bi, i: (i, 0)),
        ],
        out_specs=[tile, tile, pl.BlockSpec((1, TM, 2 * V_W), lambda bi, i: (bi, i, 0)), tile],
        out_shape=[out, out, jax.ShapeDtypeStruct((b, t, 2 * V_W), BF16), out],
        compiler_params=_params("parallel", "parallel"),
        name="inproj",
    )(x, x, x, mod3, g, w_bf, conv_w, tab)


def _ctx_kernel(x_ref, mod_ref, g_ref, wk_ref, wv_ref, k_ref, v_ref):
    sh = mod_ref[0, :, 0:D_MODEL]
    sc = mod_ref[0, :, D_MODEL:2 * D_MODEL]
    xb = (_rms(x_ref[0], g_ref[...]) * (1.0 + sc) + sh).astype(BF16)
    k_ref[0] = jnp.dot(xb, wk_ref[...], preferred_element_type=F32).astype(BF16)
    _store_values(v_ref, jnp.dot(xb, wv_ref[...], preferred_element_type=F32))


def _ctxproj(ctx, mod3, g, w_bf):
    b = ctx.shape[0]
    out = jax.ShapeDtypeStruct((b, CTX_LEN, QK_W), BF16)
    tile = pl.BlockSpec((1, CTX_LEN, QK_W), lambda bi: (bi, 0, 0))
    return pl.pallas_call(
        _ctx_kernel,
        grid=(b,),
        in_specs=[
            pl.BlockSpec((1, CTX_LEN, D_MODEL), lambda bi: (bi, 0, 0)),
            pl.BlockSpec((1, 1, 6 * D_MODEL), lambda bi: (b, 0, 0)),
            pl.BlockSpec((1, D_MODEL), lambda bi: (0, 0)),
            pl.BlockSpec((D_MODEL, QK_W), lambda bi: (0, 1)),
            pl.BlockSpec((D_MODEL, V_W), lambda bi: (0, 2)),
        ],
        out_specs=[tile, pl.BlockSpec((1, CTX_LEN, 2 * V_W), lambda bi: (bi, 0, 0))],
        out_shape=[out, jax.ShapeDtypeStruct((b, CTX_LEN, 2 * V_W), BF16)],
        compiler_params=_params("parallel"),
        name="ctxproj",
    )(ctx, mod3, g, w_bf, w_bf)


def _attn_kernel(q_ref, k_ref, v_ref, kc_ref, vc_ref, lq1_ref, lk1_ref, lq2_ref, lk2_ref,
                 sg_ref, o_ref, *, lam_init):
    lam = (jnp.exp(jnp.sum(lq1_ref[...] * lk1_ref[...], keepdims=True))
           - jnp.exp(jnp.sum(lq2_ref[...] * lk2_ref[...], keepdims=True)) + lam_init)
    lane = lax.broadcasted_iota(I32, (1, V_DIM), 1)
    first = (lane < HEAD_DIM).astype(BF16)
    second = (lane >= HEAD_DIM).astype(BF16)
    sg = sg_ref[...] * (1.0 - lam_init)

    for h, u in [(h, u) for u in range(TQ_BLOCK // TQ) for h in range(HEADS)]:
        sl = slice(h * V_DIM, (h + 1) * V_DIM)
        sv = slice(h * 2 * V_DIM, (h + 1) * 2 * V_DIM)
        rows = slice(u * TQ, (u + 1) * TQ)
        qh = q_ref[0, rows, sl]
        q2 = jnp.concatenate([qh * first, qh * second], axis=0)
        s_l = lax.dot_general(q2, k_ref[0, :, sl], NT_DIMS, preferred_element_type=F32)
        s_c = lax.dot_general(q2, kc_ref[0, :, sl], NT_DIMS, preferred_element_type=F32)
        m = jnp.maximum(jnp.max(s_l, axis=1, keepdims=True), jnp.max(s_c, axis=1, keepdims=True))
        p_l = jnp.exp2(s_l - m).astype(BF16)
        p_c = jnp.exp2(s_c - m).astype(BF16)
        pv = (jnp.dot(p_l, v_ref[0, :, sv], preferred_element_type=F32)
              + jnp.dot(p_c, vc_ref[0, :, sv], preferred_element_type=F32))
        num = pv[:, :V_DIM]
        den = pv[:, V_DIM:V_DIM + 1]
        o = num[:TQ] * (1.0 / den[:TQ]) - num[TQ:] * (lam / den[TQ:])
        o_ref[0, rows, sl] = _rms(o, sg).astype(BF16)


def _attn(q, k, v, kc, vc, lq1, lk1, lq2, lk2, sg, lam_init):
    b, t, _ = q.shape
    qt = pl.BlockSpec((1, TQ_BLOCK, QK_W), lambda bi, i: (bi, i, 0))
    vec = pl.BlockSpec((1, HEAD_DIM), lambda bi, i: (0, 0))

    def per_batch(rows, width):
        return pl.BlockSpec((1, rows, width), lambda bi, i: (bi, 0, 0))

    return pl.pallas_call(
        functools.partial(_attn_kernel, lam_init=lam_init),
        grid=(b, t // TQ_BLOCK),
        in_specs=[qt, per_batch(t, QK_W), per_batch(t, 2 * V_W),
                  per_batch(CTX_LEN, QK_W), per_batch(CTX_LEN, 2 * V_W),
                  vec, vec, vec, vec, pl.BlockSpec((1, V_DIM), lambda bi, i: (0, 0))],
        out_specs=qt,
        out_shape=jax.ShapeDtypeStruct((b, t, V_W), BF16),
        compiler_params=_params("parallel", "parallel"),
        name="attn",
    )(q, k, v, kc, vc, lq1, lk1, lq2, lk2, sg)


def _out_kernel(a_ref, c_ref, x_ref, mod_ref, gpost_ref, gpre_ref, wo_ref, wr_ref,
                h_ref, xn_ref, aff_ref):
    mix = (jnp.dot(a_ref[0], wo_ref[0:V_W, :], preferred_element_type=F32)
           + jnp.dot(c_ref[0], wo_ref[V_W:V_W + CONV_W, :], preferred_element_type=F32))
    g1 = mod_ref[0, :, 2 * D_MODEL:3 * D_MODEL]
    sh2 = mod_ref[0, :, 3 * D_MODEL:4 * D_MODEL]
    sc2 = mod_ref[0, :, 4 * D_MODEL:5 * D_MODEL]
    h1 = x_ref[0] + g1 * _rms(mix, gpost_ref[...])
    h_ref[0] = h1
    xn2 = _rms(h1, gpre_ref[...]) * (1.0 + sc2) + sh2
    _store_token_major(xn_ref, (0,), xn2)
    def split(a):
        hi = a.astype(BF16)
        return hi, (a - hi.astype(F32)).astype(BF16)

    w_hi, w_lo = split(wr_ref[...])
    x_hi, x_lo = split(xn2)
    both = lax.dot_general(jnp.concatenate([w_hi, w_lo], axis=0), x_hi, NT_DIMS,
                           preferred_element_type=F32)
    logits = (both[:N_EXPERTS] + both[N_EXPERTS:]
              + lax.dot_general(w_hi, x_lo, NT_DIMS, preferred_element_type=F32))
    e = jnp.exp(logits - jnp.max(logits, axis=0, keepdims=True))
    aff_ref[0] = e / jnp.sum(e, axis=0, keepdims=True)


def _outproj(attn, conv, x, mod3, gpost, gpre, wo_bf, wr_t):
    b, t, _ = x.shape
    half = pl.BlockSpec((1, TM, V_W), lambda bi, i: (bi, i, 0))
    tile = pl.BlockSpec((1, TM, D_MODEL), lambda bi, i: (bi, i, 0))
    row = pl.BlockSpec((1, D_MODEL), lambda bi, i: (0, 0))
    act = jax.ShapeDtypeStruct((b, t, D_MODEL), F32)
    return pl.pallas_call(
        _out_kernel,
        grid=(b, t // TM),
        in_specs=[half, half, tile,
                  pl.BlockSpec((1, 1, 6 * D_MODEL), lambda bi, i: (bi, 0, 0)),
                  row, row,
                  pl.BlockSpec((V_W + CONV_W, D_MODEL), lambda bi, i: (0, 0)),
                  pl.BlockSpec((N_EXPERTS, D_MODEL), lambda bi, i: (0, 0))],
        out_specs=[tile, pl.BlockSpec((1, TM * ROW_TILES, LANES), lambda bi, i: (bi, i, 0)),
                   pl.BlockSpec((1, N_EXPERTS, TM), lambda bi, i: (bi, 0, i))],
        out_shape=[act, jax.ShapeDtypeStruct((b, t * ROW_TILES, LANES), F32),
                   jax.ShapeDtypeStruct((b, N_EXPERTS, t), F32)],
        compiler_params=_params("parallel", "parallel"),
        name="outproj",
    )(attn, conv, x, mod3, gpost, gpre, wo_bf, wr_t)


def _prefix_exclusive(mask_f):
    r = lax.broadcasted_iota(I32, (LANES, LANES), 0)
    c = lax.broadcasted_iota(I32, (LANES, LANES), 1)
    upper = (r <= c).astype(BF16)
    off = jnp.zeros((mask_f.shape[0], 1), F32)
    outs = []
    for j in range(mask_f.shape[1] // LANES):
        blk = mask_f[:, j * LANES:(j + 1) * LANES]
        inc = jnp.dot(blk.astype(BF16), upper, preferred_element_type=F32)
        outs.append(inc - blk + off)
        off = off + inc[:, LANES - 1:LANES]
    return jnp.concatenate(outs, axis=1)


def _topk_kernel(aff_ref, idx_ref, gate_ref, *, cap):
    aff = aff_ref[0]
    n_e, t = aff.shape

    def search(i, thr_bits):
        cand = thr_bits | (jnp.int32(1) << (29 - i))
        cnt = jnp.sum((aff >= pltpu.bitcast(cand, F32)).astype(F32), axis=1, keepdims=True)
        return jnp.where(cnt >= cap, cand, thr_bits)

    thr = pltpu.bitcast(lax.fori_loop(0, 30, search, jnp.zeros((n_e, 1), I32)), F32)
    gt = aff > thr
    eq = aff == thr
    need = cap - jnp.sum(gt.astype(F32), axis=1, keepdims=True)
    eq_f = eq.astype(F32)
    take_eq = jnp.where(_prefix_exclusive(eq_f) < need, eq_f, 0.0)
    sel_f = jnp.where(gt, 1.0, take_eq)
    slot = jnp.where(sel_f > 0.0, _prefix_exclusive(sel_f), -1.0)

    tok = lax.broadcasted_iota(I32, (1, t), 1)
    tok_hi = (tok >> TOK_RADIX_BITS).astype(F32)
    tok_lo = (tok & (TOK_RADIX - 1)).astype(F32)
    sub = lax.broadcasted_iota(I32, (8, t), 0)
    slot_b = slot.astype(BF16)
    slot_ids = lax.broadcasted_iota(I32, (cap, 1), 0).astype(F32).astype(BF16)
    one_b = jnp.ones((cap, t), BF16)
    zero_b = jnp.zeros((cap, t), BF16)
    lane = lax.broadcasted_iota(I32, (cap, LANES), 1)
    idx_acc = jnp.zeros((cap, LANES), F32)
    gate_acc = jnp.zeros((cap, LANES), F32)
    for e in range(n_e):
        g = aff[e:e + 1, :]
        g_hi = g.astype(BF16).astype(F32)
        g_mid = (g - g_hi).astype(BF16).astype(F32)
        g_lo = g - g_hi - g_mid
        cols = jnp.zeros((8, t), F32)
        for k, row in enumerate((tok_hi, tok_lo, g_hi, g_mid, g_lo)):
            cols = jnp.where(sub == k, row, cols)
        hit = jnp.where(slot_b[e:e + 1, :] == slot_ids, one_b, zero_b)
        picked = lax.dot_general(hit, cols.astype(BF16), NT_DIMS, preferred_element_type=F32)
        idx_col = picked[:, 0:1] * TOK_RADIX + picked[:, 1:2]
        gate_col = picked[:, 2:3] + picked[:, 3:4] + picked[:, 4:5]
        idx_acc = jnp.where(lane == e, idx_col, idx_acc)
        gate_acc = jnp.where(lane == e, gate_col, gate_acc)
    idx_ref[0] = idx_acc.astype(I32)
    gate_ref[0] = gate_acc


def _topk(aff_t, cap):
    b, n_e, t = aff_t.shape
    out = pl.BlockSpec((1, cap, LANES), lambda bi: (bi, 0, 0))
    return pl.pallas_call(
        functools.partial(_topk_kernel, cap=cap),
        grid=(b,),
        in_specs=[pl.BlockSpec((1, n_e, t), lambda bi: (bi, 0, 0))],
        out_specs=[out, out],
        out_shape=[jax.ShapeDtypeStruct((b, cap, LANES), I32),
                   jax.ShapeDtypeStruct((b, cap, LANES), F32)],
        compiler_params=_params("parallel"),
        name="topk",
    )(aff_t)


def _ffn_kernel(idx_ref, idx_next_ref, xn_hbm, wg_ref, wu_ref, wd_ref, y_ref, xs_ref, sems, *, cap):
    nb = pl.num_programs(1)
    step = pl.program_id(0) * nb + pl.program_id(1)
    last = pl.num_programs(0) * nb - 1
    rows = FFN_G * cap
    slot = step % 2

    def row_copy(ids_ref, batch0, k, r, dst_slot):
        src_row = pl.multiple_of(ids_ref[k, 0, 0, r], ROW_TILES)
        return pltpu.make_async_copy(xn_hbm.at[batch0 + k, pl.ds(src_row, ROW_TILES), :],
                                     xs_ref.at[dst_slot, pl.ds((k * cap + r) * ROW_TILES, ROW_TILES), :],
                                     sems.at[dst_slot])

    def wait_slot(s):
        pltpu.make_async_copy(xn_hbm.at[0, pl.ds(0, 2 * rows * ROW_TILES), :], xs_ref.at[s], sems.at[s]).wait()

    def expert(half):
        xs = jnp.concatenate(
            [xs_ref[slot, pl.ds(half * rows * ROW_TILES + j, rows, stride=ROW_TILES), :]
             for j in range(ROW_TILES)], axis=1)
        hg = jnp.dot(xs, wg_ref[0], preferred_element_type=F32, precision=lax.Precision.DEFAULT)
        hu = jnp.dot(xs, wu_ref[0], preferred_element_type=F32, precision=lax.Precision.DEFAULT)
        y = jnp.dot(_silu(hg) * hu, wd_ref[0], preferred_element_type=F32, precision=lax.Precision.DEFAULT)
        for g in range(FFN_G):
            _store_token_major(y_ref, (half * FFN_G + g, 0), y[g * cap:(g + 1) * cap])

    @pl.when(step == 0)
    def _():
        for k in range(2 * FFN_G):
            def issue(r, carry, k=k):
                row_copy(idx_ref, 0, k, r, 0).start()
                return carry
            lax.fori_loop(0, cap, issue, 0)

    wait_slot(slot)
    expert(0)
    nxt_batch0 = (jnp.minimum(step + 1, last) % nb) * (2 * FFN_G)
    for k in range(2 * FFN_G):
        for r in range(cap):
            row_copy(idx_next_ref, nxt_batch0, k, r, 1 - slot).start()
    expert(1)

    @pl.when(step == last)
    def _():
        wait_slot(1 - slot)


def _ffn(rows4, xn2, wg, wu, wd, cap):
    b = xn2.shape[0]
    n_e, d, _ = wg.shape
    blk = 2 * FFN_G
    nb = b // blk
    wspec = pl.BlockSpec((1, d, d), lambda e, bg: (e, 0, 0))

    def next_ids(e, bg):
        nxt = jnp.minimum(e * nb + bg + 1, n_e * nb - 1)
        return (nxt % nb, nxt // nb, 0, 0)

    return pl.pallas_call(
        functools.partial(_ffn_kernel, cap=cap),
        grid=(n_e, nb),
        in_specs=[
            pl.BlockSpec((blk, 1, 1, cap), lambda e, bg: (bg, e, 0, 0), memory_space=pltpu.SMEM),
            pl.BlockSpec((blk, 1, 1, cap), next_ids, memory_space=pltpu.SMEM),
            pl.BlockSpec(memory_space=pl.ANY),
            wspec, wspec, wspec,
        ],
        out_specs=pl.BlockSpec((blk, 1, cap * ROW_TILES, LANES), lambda e, bg: (bg, e, 0, 0)),
        out_shape=jax.ShapeDtypeStruct((b, n_e, cap * ROW_TILES, LANES), F32),
        scratch_shapes=[pltpu.VMEM((2, 2 * FFN_G * cap * ROW_TILES, LANES), F32),
                        pltpu.SemaphoreType.DMA((2,))],
        compiler_params=_params("arbitrary", "arbitrary"),
        name="ffn",
    )(rows4, rows4, xn2, wg, wu, wd)


def _combine_kernel(row_ref, gate_ref, y_ref, h_ref, mod_ref, g_ref, o_ref, ff_scr, *, cap):
    group = pl.program_id(1)

    @pl.when(group == 0)
    def _():
        ff_scr[...] = jnp.zeros_like(ff_scr)

    for e in range(COMBINE_EXPERTS):
        for r0 in range(0, cap, COMBINE_UNROLL):
            dst = [pl.ds(pl.multiple_of(row_ref[0, e, 0, r0 + u], ROW_TILES), ROW_TILES)
                   for u in range(COMBINE_UNROLL)]
            sums = [ff_scr[dst[u], :]
                    + y_ref[0, e, (r0 + u) * ROW_TILES:(r0 + u + 1) * ROW_TILES, :] * gate_ref[0, e, 0, r0 + u]
                    for u in range(COMBINE_UNROLL)]
            for u in range(COMBINE_UNROLL):
                ff_scr[dst[u], :] = sums[u]

    @pl.when(group == pl.num_programs(1) - 1)
    def _():
        g2 = mod_ref[0, :, 5 * D_MODEL:6 * D_MODEL]
        g = g_ref[...]
        for c in range(h_ref.shape[1] // FINAL_ROWS):
            rows = slice(c * FINAL_ROWS, (c + 1) * FINAL_ROWS)
            ff = jnp.concatenate(
                [ff_scr[pl.ds(c * FINAL_ROWS * ROW_TILES + j, FINAL_ROWS, stride=ROW_TILES), :]
                 for j in range(ROW_TILES)], axis=1)
            o_ref[0, rows, :] = h_ref[0, rows, :] + g2 * _rms(ff, g)


def _combine(rows4, gate4, y, h1, mod3, g, cap):
    b, t, d = h1.shape
    n_e = y.shape[1]
    scalars = pl.BlockSpec((1, COMBINE_EXPERTS, 1, cap), lambda bi, e: (bi, e, 0, 0), memory_space=pltpu.SMEM)
    whole = pl.BlockSpec((1, t, d), lambda bi, e: (bi, 0, 0))
    return pl.pallas_call(
        functools.partial(_combine_kernel, cap=cap),
        grid=(b, n_e // COMBINE_EXPERTS),
        in_specs=[scalars, scalars,
                  pl.BlockSpec((1, COMBINE_EXPERTS, cap * ROW_TILES, LANES), lambda bi, e: (bi, e, 0, 0)),
                  whole,
                  pl.BlockSpec((1, 1, 6 * d), lambda bi, e: (bi, 0, 0)),
                  pl.BlockSpec((1, d), lambda bi, e: (0, 0))],
        out_specs=pl.BlockSpec((1, t, d), lambda bi, e: (bi, 0, 0), pipeline_mode=pl.Buffered(1)),
        out_shape=jax.ShapeDtypeStruct((b, t, d), F32),
        scratch_shapes=[pltpu.VMEM((t * ROW_TILES, LANES), F32)],
        compiler_params=pltpu.CompilerParams(dimension_semantics=("parallel", "arbitrary"),
                                             vmem_limit_bytes=COMBINE_VMEM_LIMIT),
        name="combine",
    )(rows4, gate4, y, h1, mod3, g)


def _rope_tables(t):
    rows = t // GRID_W
    row = jnp.repeat(jnp.arange(rows, dtype=F32), GRID_W)
    col = jnp.tile(jnp.arange(GRID_W, dtype=F32), rows)
    inv = ROPE_BASE ** (-jnp.arange(0, ROT_AXIS, 2, dtype=F32) / ROT_AXIS)
    ang_r = row[:, None] * inv[None, :]
    ang_c = col[:, None] * inv[None, :]
    ang = jnp.concatenate([ang_r, ang_r, ang_c, ang_c], axis=-1)
    ang = jnp.concatenate([ang, ang], axis=-1)
    cos, sin = jnp.cos(ang), jnp.sin(ang)
    low = (jnp.arange(LANES) % ROT_AXIS) < ROT_AXIS // 2
    sin_lo = jnp.where(low[None, :], -sin, 0.0)
    sin_hi = jnp.where(low[None, :], 0.0, sin)
    scale = jnp.float32(math.log2(math.e) / math.sqrt(HEAD_DIM))
    return jnp.concatenate([cos * scale, sin_lo * scale, sin_hi * scale, cos, sin_lo, sin_hi], axis=-1)


def kernel(x, c, ctx, c_ctx, w_ada, b_ada, norm_pre_mix, norm_post_mix, norm_pre_ffn, norm_post_ffn,
           w_in, conv_w, lambda_q1, lambda_k1, lambda_q2, lambda_k2, subln_g, w_out, w_router,
           w_gate, w_up, w_down):
    b, t, d = x.shape
    depth = w_ada.shape[0]
    assert depth == 1 and d == D_MODEL and t % TM == 0 and t % TQ_BLOCK == 0 and TQ_BLOCK % TQ == 0 and b % (2 * FFN_G) == 0
    assert b + 1 <= MOD_ROWS
    layer = 0
    lam_init = 0.8 - 0.6 * math.exp(-0.3 * layer)
    cap = CAP_FACTOR * t // N_EXPERTS

    cs = jnp.concatenate([c, c_ctx[None, :], jnp.zeros((MOD_ROWS - b - 1, d), F32)], axis=0)
    mod3 = _ada(cs, w_ada[layer], b_ada[layer][None, :]).reshape(MOD_ROWS, 1, 6 * d)

    w_in_bf = w_in[layer].astype(BF16)
    q, k, v, conv = _inproj(x, mod3, norm_pre_mix[layer][None, :], w_in_bf, conv_w[layer], _rope_tables(t))
    kc, vc = _ctxproj(ctx, mod3, norm_pre_mix[layer][None, :], w_in_bf)
    attn = _attn(q, k, v, kc, vc, lambda_q1[layer][None, :], lambda_k1[layer][None, :],
                 lambda_q2[layer][None, :], lambda_k2[layer][None, :], subln_g[layer][None, :], lam_init)
    h1, xn2, aff_t = _outproj(attn, conv, x, mod3, norm_post_mix[layer][None, :],
                              norm_pre_ffn[layer][None, :], w_out[layer].astype(BF16), w_router[layer].T)

    idx_p, gate_p = _topk(aff_t, cap)
    rows4 = (jnp.swapaxes(idx_p[:, :, :N_EXPERTS], 1, 2) * ROW_TILES).reshape(b, N_EXPERTS, 1, cap)
    gate4 = jnp.swapaxes(gate_p[:, :, :N_EXPERTS], 1, 2).reshape(b, N_EXPERTS, 1, cap)

    y = _ffn(rows4, xn2, w_gate[layer], w_up[layer], w_down[layer], cap)
    return _combine(rows4, gate4, y, h1, mod3, norm_post_ffn[layer][None, :], cap)
```

```python
import functools
import math

import jax
import jax.numpy as jnp
from jax import lax
from jax.experimental import pallas as pl
from jax.experimental.pallas import tpu as pltpu

F32 = jnp.float32
BF16 = jnp.bfloat16
I32 = jnp.int32

D_MODEL = 1024
CTX_LEN = 256
GRID_W = 64
HEADS = 4
HEAD_DIM = 64
V_DIM = 2 * HEAD_DIM
QK_W = HEADS * 2 * HEAD_DIM
V_W = HEADS * V_DIM
CONV_W = 512
PROJ_W = 2 * QK_W + V_W + 3 * CONV_W
ROPE_BASE = 10000.0
ROT_AXIS = HEAD_DIM // 2
N_EXPERTS = 16
CAP_FACTOR = 2
EPS = 1e-6
LANES = 128
ROW_TILES = D_MODEL // LANES
MOD_ROWS = 40

TM = 1024
TQ = 256
TQ_BLOCK = 1024
FFN_G = 2
TOK_RADIX_BITS = 6
TOK_RADIX = 1 << TOK_RADIX_BITS
COMBINE_UNROLL = 8
COMBINE_EXPERTS = 4
FINAL_ROWS = 256
COMBINE_VMEM_LIMIT = 56 * 1024 * 1024
VMEM_LIMIT = 48 * 1024 * 1024
HIGHEST = lax.Precision.HIGHEST
NT_DIMS = (((1,), (1,)), ((), ()))


def _params(*sem):
    return pltpu.CompilerParams(dimension_semantics=sem, vmem_limit_bytes=VMEM_LIMIT)


def _rms(x, g):
    return x * lax.rsqrt(jnp.mean(x * x, axis=-1, keepdims=True) + EPS) * g


def _silu(x):
    return x * jax.nn.sigmoid(x)


def _store_token_major(ref, lead, x):
    for j in range(ROW_TILES):
        ref[lead + (pl.ds(j, x.shape[0], stride=ROW_TILES), slice(None))] = x[:, j * LANES:(j + 1) * LANES]


def _load_token_major(ref, lead, rows):
    return jnp.concatenate([ref[lead + (pl.ds(j, rows, stride=ROW_TILES), slice(None))]
                            for j in range(ROW_TILES)], axis=1)


def _ada_kernel(cs_ref, w_ref, b_ref, o_ref):
    s = _silu(cs_ref[...])
    o_ref[...] = jnp.dot(s, w_ref[...], preferred_element_type=F32, precision=HIGHEST) + b_ref[...]


def _ada(cs, w, b):
    n = w.shape[1]
    tn = 512
    return pl.pallas_call(
        _ada_kernel,
        grid=(n // tn,),
        in_specs=[pl.BlockSpec((MOD_ROWS, D_MODEL), lambda j: (0, 0)),
                  pl.BlockSpec((D_MODEL, tn), lambda j: (0, j)),
                  pl.BlockSpec((1, tn), lambda j: (0, j))],
        out_specs=pl.BlockSpec((MOD_ROWS, tn), lambda j: (0, j)),
        out_shape=jax.ShapeDtypeStruct((MOD_ROWS, n), F32),
        compiler_params=_params("parallel"),
        name="ada",
    )(cs, w, b)


def _rope_store(p, tabs, out_ref):
    cos, sin_lo, sin_hi = tabs
    for j in range(QK_W // LANES):
        xj = p[:, j * LANES:(j + 1) * LANES]
        r = xj * cos + pltpu.roll(xj, LANES - 16, 1) * sin_lo + pltpu.roll(xj, 16, 1) * sin_hi
        out_ref[0, :, j * LANES:(j + 1) * LANES] = r.astype(BF16)


def _store_values(v_ref, pv):
    ones = jnp.ones((pv.shape[0], V_DIM), BF16)
    for h in range(HEADS):
        v_ref[0, :, 2 * h * V_DIM:(2 * h + 1) * V_DIM] = pv[:, h * V_DIM:(h + 1) * V_DIM].astype(BF16)
        v_ref[0, :, (2 * h + 1) * V_DIM:(2 * h + 2) * V_DIM] = ones


def _in_kernel(x_ref, xp_ref, xn_ref, mod_ref, g_ref, w_ref, cw_ref, tab_ref,
               q_ref, k_ref, v_ref, c_ref):
    i = pl.program_id(1)
    last = pl.num_programs(1) - 1
    g = g_ref[...]
    sh = mod_ref[0, :, 0:D_MODEL]
    sc = mod_ref[0, :, D_MODEL:2 * D_MODEL]

    def norm_mod(xt):
        return (_rms(xt, g) * (1.0 + sc) + sh).astype(BF16)

    def proj(xb, c0, width):
        return jnp.dot(xb, w_ref[:, c0:c0 + width], preferred_element_type=F32)

    xb = norm_mod(x_ref[0])
    tabs = [tab_ref[:, j * LANES:(j + 1) * LANES] for j in range(6)]
    _rope_store(proj(xb, 0, QK_W), tabs[0:3], q_ref)
    _rope_store(proj(xb, QK_W, QK_W), tabs[3:6], k_ref)
    _store_values(v_ref, proj(xb, 2 * QK_W, V_W))

    c0 = 2 * QK_W + V_W
    gb = proj(xb, c0, CONV_W)
    u = proj(xb, c0 + CONV_W, CONV_W) * proj(xb, c0 + 2 * CONV_W, CONV_W)
    xh = norm_mod(jnp.concatenate([xp_ref[0], xn_ref[0]], axis=0))
    ph = proj(xh, c0 + CONV_W, 2 * CONV_W)
    uh = ph[:, :CONV_W] * ph[:, CONV_W:]
    u_before = jnp.where(i > 0, uh[7:8], 0.0)
    u_after = jnp.where(i < last, uh[8:9], 0.0)
    rows = lax.broadcasted_iota(I32, u.shape, 0)
    u_prev = jnp.where(rows == 0, u_before, pltpu.roll(u, 1, 0))
    u_next = jnp.where(rows == TM - 1, u_after, pltpu.roll(u, TM - 1, 0))
    cw = cw_ref[...]
    y = cw[0:1] * u_prev + cw[1:2] * u + cw[2:3] * u_next
    c_ref[0] = (gb * y).astype(BF16)


def _inproj(x, mod3, g, w_bf, conv_w, tab):
    b, t, _ = x.shape
    nt = t // TM
    hb = TM // 8
    out = jax.ShapeDtypeStruct((b, t, QK_W), BF16)
    tile = pl.BlockSpec((1, TM, QK_W), lambda bi, i: (bi, i, 0))
    return pl.pallas_call(
        _in_kernel,
        grid=(b, nt),
        in_specs=[
            pl.BlockSpec((1, TM, D_MODEL), lambda bi, i: (bi, i, 0)),
            pl.BlockSpec((1, 8, D_MODEL), lambda bi, i: (bi, jnp.maximum(i * hb - 1, 0), 0)),
            pl.BlockSpec((1, 8, D_MODEL), lambda bi, i: (bi, jnp.minimum((i + 1) * hb, t // 8 - 1), 0)),
            pl.BlockSpec((1, 1, 6 * D_MODEL), lambda bi, i: (bi, 0, 0)),
            pl.BlockSpec((1, D_MODEL), lambda bi, i: (0, 0)),
            pl.BlockSpec((D_MODEL, PROJ_W), lambda bi, i: (0, 0)),
            pl.BlockSpec((3, CONV_W), lambda bi, i: (0, 0)),
            pl.BlockSpec((TM, 6 * LANES), lambda bi, i: (i, 0)),
        ],
        out_specs=[tile, tile, pl.BlockSpec((1, TM, 2 * V_W), lambda bi, i: (bi, i, 0)), tile],
        out_shape=[out, out, jax.ShapeDtypeStruct((b, t, 2 * V_W), BF16), out],
        compiler_params=_params("parallel", "parallel"),
        name="inproj",
    )(x, x, x, mod3, g, w_bf, conv_w, tab)


def _ctx_kernel(x_ref, mod_ref, g_ref, wk_ref, wv_ref, k_ref, v_ref):
    sh = mod_ref[0, :, 0:D_MODEL]
    sc = mod_ref[0, :, D_MODEL:2 * D_MODEL]
    xb = (_rms(x_ref[0], g_ref[...]) * (1.0 + sc) + sh).astype(BF16)
    k_ref[0] = jnp.dot(xb, wk_ref[...], preferred_element_type=F32).astype(BF16)
    _store_values(v_ref, jnp.dot(xb, wv_ref[...], preferred_element_type=F32))


def _ctxproj(ctx, mod3, g, w_bf):
    b = ctx.shape[0]
    out = jax.ShapeDtypeStruct((b, CTX_LEN, QK_W), BF16)
    tile = pl.BlockSpec((1, CTX_LEN, QK_W), lambda bi: (bi, 0, 0))
    return pl.pallas_call(
        _ctx_kernel,
        grid=(b,),
        in_specs=[
            pl.BlockSpec((1, CTX_LEN, D_MODEL), lambda bi: (bi, 0, 0)),
            pl.BlockSpec((1, 1, 6 * D_MODEL), lambda bi: (b, 0, 0)),
            pl.BlockSpec((1, D_MODEL), lambda bi: (0, 0)),
            pl.BlockSpec((D_MODEL, QK_W), lambda bi: (0, 1)),
            pl.BlockSpec((D_MODEL, V_W), lambda bi: (0, 2)),
        ],
        out_specs=[tile, pl.BlockSpec((1, CTX_LEN, 2 * V_W), lambda bi: (bi, 0, 0))],
        out_shape=[out, jax.ShapeDtypeStruct((b, CTX_LEN, 2 * V_W), BF16)],
        compiler_params=_params("parallel"),
        name="ctxproj",
    )(ctx, mod3, g, w_bf, w_bf)


def _attn_kernel(q_ref, k_ref, v_ref, kc_ref, vc_ref, lq1_ref, lk1_ref, lq2_ref, lk2_ref,
                 sg_ref, o_ref, *, lam_init):
    lam = (jnp.exp(jnp.sum(lq1_ref[...] * lk1_ref[...], keepdims=True))
           - jnp.exp(jnp.sum(lq2_ref[...] * lk2_ref[...], keepdims=True)) + lam_init)
    lane = lax.broadcasted_iota(I32, (1, V_DIM), 1)
    first = (lane < HEAD_DIM).astype(BF16)
    second = (lane >= HEAD_DIM).astype(BF16)
    sg = sg_ref[...] * (1.0 - lam_init)

    for h, u in [(h, u) for u in range(TQ_BLOCK // TQ) for h in range(HEADS)]:
        sl = slice(h * V_DIM, (h + 1) * V_DIM)
        sv = slice(h * 2 * V_DIM, (h + 1) * 2 * V_DIM)
        rows = slice(u * TQ, (u + 1) * TQ)
        qh = q_ref[0, rows, sl]
        q2 = jnp.concatenate([qh * first, qh * second], axis=0)
        s_l = lax.dot_general(q2, k_ref[0, :, sl], NT_DIMS, preferred_element_type=F32)
        s_c = lax.dot_general(q2, kc_ref[0, :, sl], NT_DIMS, preferred_element_type=F32)
        m = jnp.maximum(jnp.max(s_l, axis=1, keepdims=True), jnp.max(s_c, axis=1, keepdims=True))
        p_l = jnp.exp2(s_l - m).astype(BF16)
        p_c = jnp.exp2(s_c - m).astype(BF16)
        pv = (jnp.dot(p_l, v_ref[0, :, sv], preferred_element_type=F32)
              + jnp.dot(p_c, vc_ref[0, :, sv], preferred_element_type=F32))
        num = pv[:, :V_DIM]
        den = pv[:, V_DIM:V_DIM + 1]
        o = num[:TQ] * (1.0 / den[:TQ]) - num[TQ:] * (lam / den[TQ:])
        o_ref[0, rows, sl] = _rms(o, sg).astype(BF16)


def _attn(q, k, v, kc, vc, lq1, lk1, lq2, lk2, sg, lam_init):
    b, t, _ = q.shape
    qt = pl.BlockSpec((1, TQ_BLOCK, QK_W), lambda bi, i: (bi, i, 0))
    vec = pl.BlockSpec((1, HEAD_DIM), lambda bi, i: (0, 0))

    def per_batch(rows, width):
        return pl.BlockSpec((1, rows, width), lambda bi, i: (bi, 0, 0))

    return pl.pallas_call(
        functools.partial(_attn_kernel, lam_init=lam_init),
        grid=(b, t // TQ_BLOCK),
        in_specs=[qt, per_batch(t, QK_W), per_batch(t, 2 * V_W),
                  per_batch(CTX_LEN, QK_W), per_batch(CTX_LEN, 2 * V_W),
                  vec, vec, vec, vec, pl.BlockSpec((1, V_DIM), lambda bi, i: (0, 0))],
        out_specs=qt,
        out_shape=jax.ShapeDtypeStruct((b, t, V_W), BF16),
        compiler_params=_params("parallel", "parallel"),
        name="attn",
    )(q, k, v, kc, vc, lq1, lk1, lq2, lk2, sg)


def _out_kernel(a_ref, c_ref, x_ref, mod_ref, gpost_ref, gpre_ref, wo_ref, wr_ref,
                h_ref, xn_ref, aff_ref):
    mix = (jnp.dot(a_ref[0], wo_ref[0:V_W, :], preferred_element_type=F32)
           + jnp.dot(c_ref[0], wo_ref[V_W:V_W + CONV_W, :], preferred_element_type=F32))
    g1 = mod_ref[0, :, 2 * D_MODEL:3 * D_MODEL]
    sh2 = mod_ref[0, :, 3 * D_MODEL:4 * D_MODEL]
    sc2 = mod_ref[0, :, 4 * D_MODEL:5 * D_MODEL]
    h1 = x_ref[0] + g1 * _rms(mix, gpost_ref[...])
    h_ref[0] = h1
    xn2 = _rms(h1, gpre_ref[...]) * (1.0 + sc2) + sh2
    _store_token_major(xn_ref, (0,), xn2)
    def split(a):
        hi = a.astype(BF16)
        return hi, (a - hi.astype(F32)).astype(BF16)

    w_hi, w_lo = split(wr_ref[...])
    x_hi, x_lo = split(xn2)
    both = lax.dot_general(jnp.concatenate([w_hi, w_lo], axis=0), x_hi, NT_DIMS,
                           preferred_element_type=F32)
    logits = (both[:N_EXPERTS] + both[N_EXPERTS:]
              + lax.dot_general(w_hi, x_lo, NT_DIMS, preferred_element_type=F32))
    e = jnp.exp(logits - jnp.max(logits, axis=0, keepdims=True))
    aff_ref[0] = e / jnp.sum(e, axis=0, keepdims=True)


def _outproj(attn, conv, x, mod3, gpost, gpre, wo_bf, wr_t):
    b, t, _ = x.shape
    half = pl.BlockSpec((1, TM, V_W), lambda bi, i: (bi, i, 0))
    tile = pl.BlockSpec((1, TM, D_MODEL), lambda bi, i: (bi, i, 0))
    row = pl.BlockSpec((1, D_MODEL), lambda bi, i: (0, 0))
    act = jax.ShapeDtypeStruct((b, t, D_MODEL), F32)
    return pl.pallas_call(
        _out_kernel,
        grid=(b, t // TM),
        in_specs=[half, half, tile,
                  pl.BlockSpec((1, 1, 6 * D_MODEL), lambda bi, i: (bi, 0, 0)),
                  row, row,
                  pl.BlockSpec((V_W + CONV_W, D_MODEL), lambda bi, i: (0, 0)),
                  pl.BlockSpec((N_EXPERTS, D_MODEL), lambda bi, i: (0, 0))],
        out_specs=[tile, pl.BlockSpec((1, TM * ROW_TILES, LANES), lambda bi, i: (bi, i, 0)),
                   pl.BlockSpec((1, N_EXPERTS, TM), lambda bi, i: (bi, 0, i))],
        out_shape=[act, jax.ShapeDtypeStruct((b, t * ROW_TILES, LANES), F32),
                   jax.ShapeDtypeStruct((b, N_EXPERTS, t), F32)],
        compiler_params=_params("parallel", "parallel"),
        name="outproj",
    )(attn, conv, x, mod3, gpost, gpre, wo_bf, wr_t)


def _prefix_exclusive(mask_f):
    r = lax.broadcasted_iota(I32, (LANES, LANES), 0)
    c = lax.broadcasted_iota(I32, (LANES, LANES), 1)
    upper = (r <= c).astype(BF16)
    off = jnp.zeros((mask_f.shape[0], 1), F32)
    outs = []
    for j in range(mask_f.shape[1] // LANES):
        blk = mask_f[:, j * LANES:(j + 1) * LANES]
        inc = jnp.dot(blk.astype(BF16), upper, preferred_element_type=F32)
        outs.append(inc - blk + off)
        off = off + inc[:, LANES - 1:LANES]
    return jnp.concatenate(outs, axis=1)


def _topk_kernel(aff_ref, idx_ref, gate_ref, *, cap):
    aff = aff_ref[0]
    n_e, t = aff.shape

    def search(i, thr_bits):
        cand = thr_bits | (jnp.int32(1) << (29 - i))
        cnt = jnp.sum((aff >= pltpu.bitcast(cand, F32)).astype(F32), axis=1, keepdims=True)
        return jnp.where(cnt >= cap, cand, thr_bits)

    thr = pltpu.bitcast(lax.fori_loop(0, 30, search, jnp.zeros((n_e, 1), I32)), F32)
    gt = aff > thr
    eq = aff == thr
    need = cap - jnp.sum(gt.astype(F32), axis=1, keepdims=True)
    eq_f = eq.astype(F32)
    take_eq = jnp.where(_prefix_exclusive(eq_f) < need, eq_f, 0.0)
    sel_f = jnp.where(gt, 1.0, take_eq)
    slot = jnp.where(sel_f > 0.0, _prefix_exclusive(sel_f), -1.0)

    tok = lax.broadcasted_iota(I32, (1, t), 1)
    tok_hi = (tok >> TOK_RADIX_BITS).astype(F32)
    tok_lo = (tok & (TOK_RADIX - 1)).astype(F32)
    sub = lax.broadcasted_iota(I32, (8, t), 0)
    slot_b = slot.astype(BF16)
    slot_ids = lax.broadcasted_iota(I32, (cap, 1), 0).astype(F32).astype(BF16)
    one_b = jnp.ones((cap, t), BF16)
    zero_b = jnp.zeros((cap, t), BF16)
    lane = lax.broadcasted_iota(I32, (cap, LANES), 1)
    idx_acc = jnp.zeros((cap, LANES), F32)
    gate_acc = jnp.zeros((cap, LANES), F32)
    for e in range(n_e):
        g = aff[e:e + 1, :]
        g_hi = g.astype(BF16).astype(F32)
        g_mid = (g - g_hi).astype(BF16).astype(F32)
        g_lo = g - g_hi - g_mid
        cols = jnp.zeros((8, t), F32)
        for k, row in enumerate((tok_hi, tok_lo, g_hi, g_mid, g_lo)):
            cols = jnp.where(sub == k, row, cols)
        hit = jnp.where(slot_b[e:e + 1, :] == slot_ids, one_b, zero_b)
        picked = lax.dot_general(hit, cols.astype(BF16), NT_DIMS, preferred_element_type=F32)
        idx_col = picked[:, 0:1] * TOK_RADIX + picked[:, 1:2]
        gate_col = picked[:, 2:3] + picked[:, 3:4] + picked[:, 4:5]
        idx_acc = jnp.where(lane == e, idx_col, idx_acc)
        gate_acc = jnp.where(lane == e, gate_col, gate_acc)
    idx_ref[0] = idx_acc.astype(I32)
    gate_ref[0] = gate_acc


def _topk(aff_t, cap):
    b, n_e, t = aff_t.shape
    out = pl.BlockSpec((1, cap, LANES), lambda bi: (bi, 0, 0))
    return pl.pallas_call(
        functools.partial(_topk_kernel, cap=cap),
        grid=(b,),
        in_specs=[pl.BlockSpec((1, n_e, t), lambda bi: (bi, 0, 0))],
        out_specs=[out, out],
        out_shape=[jax.ShapeDtypeStruct((b, cap, LANES), I32),
                   jax.ShapeDtypeStruct((b, cap, LANES), F32)],
        compiler_params=_params("parallel"),
        name="topk",
    )(aff_t)


def _ffn_kernel(idx_ref, idx_next_ref, xn_hbm, wg_ref, wu_ref, wd_ref, y_ref, xs_ref, sems, *, cap):
    nb = pl.num_programs(1)
    step = pl.program_id(0) * nb + pl.program_id(1)
    last = pl.num_programs(0) * nb - 1
    rows = FFN_G * cap
    slot = step % 2

    def row_copy(ids_ref, batch0, k, r, dst_slot):
        src_row = pl.multiple_of(ids_ref[k, 0, 0, r], ROW_TILES)
        return pltpu.make_async_copy(xn_hbm.at[batch0 + k, pl.ds(src_row, ROW_TILES), :],
                                     xs_ref.at[dst_slot, pl.ds((k * cap + r) * ROW_TILES, ROW_TILES), :],
                                     sems.at[dst_slot])

    def wait_slot(s):
        pltpu.make_async_copy(xn_hbm.at[0, pl.ds(0, 2 * rows * ROW_TILES), :], xs_ref.at[s], sems.at[s]).wait()

    def expert(half):
        xs = jnp.concatenate(
            [xs_ref[slot, pl.ds(half * rows * ROW_TILES + j, rows, stride=ROW_TILES), :]
             for j in range(ROW_TILES)], axis=1)
        hg = jnp.dot(xs, wg_ref[0], preferred_element_type=F32, precision=lax.Precision.DEFAULT)
        hu = jnp.dot(xs, wu_ref[0], preferred_element_type=F32, precision=lax.Precision.DEFAULT)
        y = jnp.dot(_silu(hg) * hu, wd_ref[0], preferred_element_type=F32, precision=lax.Precision.DEFAULT)
        for g in range(FFN_G):
            _store_token_major(y_ref, (half * FFN_G + g, 0), y[g * cap:(g + 1) * cap])

    @pl.when(step == 0)
    def _():
        for k in range(2 * FFN_G):
            def issue(r, carry, k=k):
                row_copy(idx_ref, 0, k, r, 0).start()
                return carry
            lax.fori_loop(0, cap, issue, 0)

    wait_slot(slot)
    expert(0)
    nxt_batch0 = (jnp.minimum(step + 1, last) % nb) * (2 * FFN_G)
    for k in range(2 * FFN_G):
        for r in range(cap):
            row_copy(idx_next_ref, nxt_batch0, k, r, 1 - slot).start()
    expert(1)

    @pl.when(step == last)
    def _():
        wait_slot(1 - slot)


def _ffn(rows4, xn2, wg, wu, wd, cap):
    b = xn2.shape[0]
    n_e, d, _ = wg.shape
    blk = 2 * FFN_G
    nb = b // blk
    wspec = pl.BlockSpec((1, d, d), lambda e, bg: (e, 0, 0))

    def next_ids(e, bg):
        nxt = jnp.minimum(e * nb + bg + 1, n_e * nb - 1)
        return (nxt % nb, nxt // nb, 0, 0)

    return pl.pallas_call(
        functools.partial(_ffn_kernel, cap=cap),
        grid=(n_e, nb),
        in_specs=[
            pl.BlockSpec((blk, 1, 1, cap), lambda e, bg: (bg, e, 0, 0), memory_space=pltpu.SMEM),
            pl.BlockSpec((blk, 1, 1, cap), next_ids, memory_space=pltpu.SMEM),
            pl.BlockSpec(memory_space=pl.ANY),
            wspec, wspec, wspec,
        ],
        out_specs=pl.BlockSpec((blk, 1, cap * ROW_TILES, LANES), lambda e, bg: (bg, e, 0, 0)),
        out_shape=jax.ShapeDtypeStruct((b, n_e, cap * ROW_TILES, LANES), F32),
        scratch_shapes=[pltpu.VMEM((2, 2 * FFN_G * cap * ROW_TILES, LANES), F32),
                        pltpu.SemaphoreType.DMA((2,))],
        compiler_params=_params("arbitrary", "arbitrary"),
        name="ffn",
    )(rows4, rows4, xn2, wg, wu, wd)


def _combine_kernel(row_ref, gate_ref, y_ref, h_hbm, mod_ref, g_ref, o_ref, ff_scr, h_scr, h_sem, *, cap):
    group = pl.program_id(1)
    h_copy = pltpu.make_async_copy(h_hbm.at[pl.program_id(0)], h_scr, h_sem)

    @pl.when(group == 0)
    def _():
        h_copy.start()
        ff_scr[...] = jnp.zeros_like(ff_scr)

    for e in range(COMBINE_EXPERTS):
        for r0 in range(0, cap, COMBINE_UNROLL):
            dst = [pl.ds(pl.multiple_of(row_ref[0, e, 0, r0 + u], ROW_TILES), ROW_TILES)
                   for u in range(COMBINE_UNROLL)]
            sums = [ff_scr[dst[u], :]
                    + y_ref[0, e, (r0 + u) * ROW_TILES:(r0 + u + 1) * ROW_TILES, :] * gate_ref[0, e, 0, r0 + u]
                    for u in range(COMBINE_UNROLL)]
            for u in range(COMBINE_UNROLL):
                ff_scr[dst[u], :] = sums[u]

    @pl.when(group == pl.num_programs(1) - 1)
    def _():
        h_copy.wait()
        g2 = mod_ref[0, :, 5 * D_MODEL:6 * D_MODEL]
        g = g_ref[...]
        for c in range(h_scr.shape[0] // FINAL_ROWS):
            rows = slice(c * FINAL_ROWS, (c + 1) * FINAL_ROWS)
            ff = jnp.concatenate(
                [ff_scr[pl.ds(c * FINAL_ROWS * ROW_TILES + j, FINAL_ROWS, stride=ROW_TILES), :]
                 for j in range(ROW_TILES)], axis=1)
            o_ref[0, rows, :] = h_scr[rows, :] + g2 * _rms(ff, g)


def _combine(rows4, gate4, y, h1, mod3, g, cap):
    b, t, d = h1.shape
    n_e = y.shape[1]
    scalars = pl.BlockSpec((1, COMBINE_EXPERTS, 1, cap), lambda bi, e: (bi, e, 0, 0), memory_space=pltpu.SMEM)
    return pl.pallas_call(
        functools.partial(_combine_kernel, cap=cap),
        grid=(b, n_e // COMBINE_EXPERTS),
        in_specs=[scalars, scalars,
                  pl.BlockSpec((1, COMBINE_EXPERTS, cap * ROW_TILES, LANES), lambda bi, e: (bi, e, 0, 0)),
                  pl.BlockSpec(memory_space=pl.ANY),
                  pl.BlockSpec((1, 1, 6 * d), lambda bi, e: (bi, 0, 0)),
                  pl.BlockSpec((1, d), lambda bi, e: (0, 0))],
        out_specs=pl.BlockSpec((1, t, d), lambda bi, e: (bi, 0, 0)),
        out_shape=jax.ShapeDtypeStruct((b, t, d), F32),
        scratch_shapes=[pltpu.VMEM((t * ROW_TILES, LANES), F32), pltpu.VMEM((t, d), F32),
                        pltpu.SemaphoreType.DMA(())],
        compiler_params=pltpu.CompilerParams(dimension_semantics=("parallel", "arbitrary"),
                                             vmem_limit_bytes=COMBINE_VMEM_LIMIT),
        name="combine",
    )(rows4, gate4, y, h1, mod3, g)


def _rope_tables(t):
    rows = t // GRID_W
    row = jnp.repeat(jnp.arange(rows, dtype=F32), GRID_W)
    col = jnp.tile(jnp.arange(GRID_W, dtype=F32), rows)
    inv = ROPE_BASE ** (-jnp.arange(0, ROT_AXIS, 2, dtype=F32) / ROT_AXIS)
    ang_r = row[:, None] * inv[None, :]
    ang_c = col[:, None] * inv[None, :]
    ang = jnp.concatenate([ang_r, ang_r, ang_c, ang_c], axis=-1)
    ang = jnp.concatenate([ang, ang], axis=-1)
    cos, sin = jnp.cos(ang), jnp.sin(ang)
    low = (jnp.arange(LANES) % ROT_AXIS) < ROT_AXIS // 2
    sin_lo = jnp.where(low[None, :], -sin, 0.0)
    sin_hi = jnp.where(low[None, :], 0.0, sin)
    scale = jnp.float32(math.log2(math.e) / math.sqrt(HEAD_DIM))
    return jnp.concatenate([cos * scale, sin_lo * scale, sin_hi * scale, cos, sin_lo, sin_hi], axis=-1)


def kernel(x, c, ctx, c_ctx, w_ada, b_ada, norm_pre_mix, norm_post_mix, norm_pre_ffn, norm_post_ffn,
           w_in, conv_w, lambda_q1, lambda_k1, lambda_q2, lambda_k2, subln_g, w_out, w_router,
           w_gate, w_up, w_down):
    b, t, d = x.shape
    depth = w_ada.shape[0]
    assert depth == 1 and d == D_MODEL and t % TM == 0 and t % TQ_BLOCK == 0 and TQ_BLOCK % TQ == 0 and b % (2 * FFN_G) == 0
    assert b + 1 <= MOD_ROWS
    layer = 0
    lam_init = 0.8 - 0.6 * math.exp(-0.3 * layer)
    cap = CAP_FACTOR * t // N_EXPERTS

    cs = jnp.concatenate([c, c_ctx[None, :], jnp.zeros((MOD_ROWS - b - 1, d), F32)], axis=0)
    mod3 = _ada(cs, w_ada[layer], b_ada[layer][None, :]).reshape(MOD_ROWS, 1, 6 * d)

    w_in_bf = w_in[layer].astype(BF16)
    q, k, v, conv = _inproj(x, mod3, norm_pre_mix[layer][None, :], w_in_bf, conv_w[layer], _rope_tables(t))
    kc, vc = _ctxproj(ctx, mod3, norm_pre_mix[layer][None, :], w_in_bf)
    attn = _attn(q, k, v, kc, vc, lambda_q1[layer][None, :], lambda_k1[layer][None, :],
                 lambda_q2[layer][None, :], lambda_k2[layer][None, :], subln_g[layer][None, :], lam_init)
    h1, xn2, aff_t = _outproj(attn, conv, x, mod3, norm_post_mix[layer][None, :],
                              norm_pre_ffn[layer][None, :], w_out[layer].astype(BF16), w_router[layer].T)

    idx_p, gate_p = _topk(aff_t, cap)
    rows4 = (jnp.swapaxes(idx_p[:, :, :N_EXPERTS], 1, 2) * ROW_TILES).reshape(b, N_EXPERTS, 1, cap)
    gate4 = jnp.swapaxes(gate_p[:, :, :N_EXPERTS], 1, 2).reshape(b, N_EXPERTS, 1, cap)

    y = _ffn(rows4, xn2, w_gate[layer], w_up[layer], w_down[layer], cap)
    return _combine(rows4, gate4, y, h1, mod3, norm_post_ffn[layer][None, :], cap)
```

```python
import functools
import math

import jax
import jax.numpy as jnp
from jax import lax
from jax.experimental import pallas as pl
from jax.experimental.pallas import tpu as pltpu

F32 = jnp.float32
BF16 = jnp.bfloat16
I32 = jnp.int32

D_MODEL = 1024
CTX_LEN = 256
GRID_W = 64
HEADS = 4
HEAD_DIM = 64
V_DIM = 2 * HEAD_DIM
QK_W = HEADS * 2 * HEAD_DIM
V_W = HEADS * V_DIM
CONV_W = 512
PROJ_W = 2 * QK_W + V_W + 3 * CONV_W
ROPE_BASE = 10000.0
ROT_AXIS = HEAD_DIM // 2
N_EXPERTS = 16
CAP_FACTOR = 2
EPS = 1e-6
LANES = 128
ROW_TILES = D_MODEL // LANES
MOD_ROWS = 40

TM = 1024
TQ = 256
TQ_BLOCK = 1024
FFN_G = 2
TOK_RADIX_BITS = 6
TOK_RADIX = 1 << TOK_RADIX_BITS
COMBINE_UNROLL = 8
COMBINE_EXPERTS = 4
FINAL_ROWS = 256
COMBINE_VMEM_LIMIT = 56 * 1024 * 1024
VMEM_LIMIT = 48 * 1024 * 1024
HIGHEST = lax.Precision.HIGHEST
NT_DIMS = (((1,), (1,)), ((), ()))


def _params(*sem):
    return pltpu.CompilerParams(dimension_semantics=sem, vmem_limit_bytes=VMEM_LIMIT)


def _rms(x, g):
    return x * lax.rsqrt(jnp.mean(x * x, axis=-1, keepdims=True) + EPS) * g


def _silu(x):
    return x * jax.nn.sigmoid(x)


def _store_token_major(ref, lead, x):
    for j in range(ROW_TILES):
        ref[lead + (pl.ds(j, x.shape[0], stride=ROW_TILES), slice(None))] = x[:, j * LANES:(j + 1) * LANES]


def _load_token_major(ref, lead, rows):
    return jnp.concatenate([ref[lead + (pl.ds(j, rows, stride=ROW_TILES), slice(None))]
                            for j in range(ROW_TILES)], axis=1)


def _ada_kernel(cs_ref, w_ref, b_ref, o_ref):
    s = _silu(cs_ref[...])
    o_ref[...] = jnp.dot(s, w_ref[...], preferred_element_type=F32, precision=HIGHEST) + b_ref[...]


def _ada(cs, w, b):
    n = w.shape[1]
    tn = 512
    return pl.pallas_call(
        _ada_kernel,
        grid=(n // tn,),
        in_specs=[pl.BlockSpec((MOD_ROWS, D_MODEL), lambda j: (0, 0)),
                  pl.BlockSpec((D_MODEL, tn), lambda j: (0, j)),
                  pl.BlockSpec((1, tn), lambda j: (0, j))],
        out_specs=pl.BlockSpec((MOD_ROWS, tn), lambda j: (0, j)),
        out_shape=jax.ShapeDtypeStruct((MOD_ROWS, n), F32),
        compiler_params=_params("parallel"),
        name="ada",
    )(cs, w, b)


def _rope_store(p, tabs, out_ref):
    cos, sin_lo, sin_hi = tabs
    for j in range(QK_W // LANES):
        xj = p[:, j * LANES:(j + 1) * LANES]
        r = xj * cos + pltpu.roll(xj, LANES - 16, 1) * sin_lo + pltpu.roll(xj, 16, 1) * sin_hi
        out_ref[0, :, j * LANES:(j + 1) * LANES] = r.astype(BF16)


def _store_values(v_ref, pv):
    ones = jnp.ones((pv.shape[0], V_DIM), BF16)
    for h in range(HEADS):
        v_ref[0, :, 2 * h * V_DIM:(2 * h + 1) * V_DIM] = pv[:, h * V_DIM:(h + 1) * V_DIM].astype(BF16)
        v_ref[0, :, (2 * h + 1) * V_DIM:(2 * h + 2) * V_DIM] = ones


def _in_kernel(x_ref, xp_ref, xn_ref, mod_ref, g_ref, w_ref, cw_ref, tab_ref,
               q_ref, k_ref, v_ref, c_ref):
    i = pl.program_id(1)
    last = pl.num_programs(1) - 1
    g = g_ref[...]
    sh = mod_ref[0, :, 0:D_MODEL]
    sc = mod_ref[0, :, D_MODEL:2 * D_MODEL]

    def norm_mod(xt):
        return (_rms(xt, g) * (1.0 + sc) + sh).astype(BF16)

    def proj(xb, c0, width):
        return jnp.dot(xb, w_ref[:, c0:c0 + width], preferred_element_type=F32)

    xb = norm_mod(x_ref[0])
    tabs = [tab_ref[:, j * LANES:(j + 1) * LANES] for j in range(6)]
    _rope_store(proj(xb, 0, QK_W), tabs[0:3], q_ref)
    _rope_store(proj(xb, QK_W, QK_W), tabs[3:6], k_ref)
    _store_values(v_ref, proj(xb, 2 * QK_W, V_W))

    c0 = 2 * QK_W + V_W
    gb = proj(xb, c0, CONV_W)
    u = proj(xb, c0 + CONV_W, CONV_W) * proj(xb, c0 + 2 * CONV_W, CONV_W)
    xh = norm_mod(jnp.concatenate([xp_ref[0], xn_ref[0]], axis=0))
    ph = proj(xh, c0 + CONV_W, 2 * CONV_W)
    uh = ph[:, :CONV_W] * ph[:, CONV_W:]
    u_before = jnp.where(i > 0, uh[7:8], 0.0)
    u_after = jnp.where(i < last, uh[8:9], 0.0)
    rows = lax.broadcasted_iota(I32, u.shape, 0)
    u_prev = jnp.where(rows == 0, u_before, pltpu.roll(u, 1, 0))
    u_next = jnp.where(rows == TM - 1, u_after, pltpu.roll(u, TM - 1, 0))
    cw = cw_ref[...]
    y = cw[0:1] * u_prev + cw[1:2] * u + cw[2:3] * u_next
    c_ref[0] = (gb * y).astype(BF16)


def _inproj(x, mod3, g, w_bf, conv_w, tab):
    b, t, _ = x.shape
    nt = t // TM
    hb = TM // 8
    out = jax.ShapeDtypeStruct((b, t, QK_W), BF16)
    tile = pl.BlockSpec((1, TM, QK_W), lambda bi, i: (bi, i, 0))
    return pl.pallas_call(
        _in_kernel,
        grid=(b, nt),
        in_specs=[
            pl.BlockSpec((1, TM, D_MODEL), lambda bi, i: (bi, i, 0)),
            pl.BlockSpec((1, 8, D_MODEL), lambda bi, i: (bi, jnp.maximum(i * hb - 1, 0), 0)),
            pl.BlockSpec((1, 8, D_MODEL), lambda bi, i: (bi, jnp.minimum((i + 1) * hb, t // 8 - 1), 0)),
            pl.BlockSpec((1, 1, 6 * D_MODEL), lambda bi, i: (bi, 0, 0)),
            pl.BlockSpec((1, D_MODEL), lambda bi, i: (0, 0)),
            pl.BlockSpec((D_MODEL, PROJ_W), lambda bi, i: (0, 0)),
            pl.BlockSpec((3, CONV_W), lambda bi, i: (0, 0)),
            pl.BlockSpec((TM, 6 * LANES), lambda bi, i: (i, 0)),
        ],
        out_specs=[tile, tile, pl.BlockSpec((1, TM, 2 * V_W), lambda bi, i: (bi, i, 0)), tile],
        out_shape=[out, out, jax.ShapeDtypeStruct((b, t, 2 * V_W), BF16), out],
        compiler_params=_params("parallel", "parallel"),
        name="inproj",
    )(x, x, x, mod3, g, w_bf, conv_w, tab)


def _ctx_kernel(x_ref, mod_ref, g_ref, wk_ref, wv_ref, k_ref, v_ref):
    sh = mod_ref[0, :, 0:D_MODEL]
    sc = mod_ref[0, :, D_MODEL:2 * D_MODEL]
    xb = (_rms(x_ref[0], g_ref[...]) * (1.0 + sc) + sh).astype(BF16)
    k_ref[0] = jnp.dot(xb, wk_ref[...], preferred_element_type=F32).astype(BF16)
    _store_values(v_ref, jnp.dot(xb, wv_ref[...], preferred_element_type=F32))


def _ctxproj(ctx, mod3, g, w_bf):
    b = ctx.shape[0]
    out = jax.ShapeDtypeStruct((b, CTX_LEN, QK_W), BF16)
    tile = pl.BlockSpec((1, CTX_LEN, QK_W), lambda bi: (bi, 0, 0))
    return pl.pallas_call(
        _ctx_kernel,
        grid=(b,),
        in_specs=[
            pl.BlockSpec((1, CTX_LEN, D_MODEL), lambda bi: (bi, 0, 0)),
            pl.BlockSpec((1, 1, 6 * D_MODEL), lambda bi: (b, 0, 0)),
            pl.BlockSpec((1, D_MODEL), lambda bi: (0, 0)),
            pl.BlockSpec((D_MODEL, QK_W), lambda bi: (0, 1)),
            pl.BlockSpec((D_MODEL, V_W), lambda bi: (0, 2)),
        ],
        out_specs=[tile, pl.BlockSpec((1, CTX_LEN, 2 * V_W), lambda bi: (bi, 0, 0))],
        out_shape=[out, jax.ShapeDtypeStruct((b, CTX_LEN, 2 * V_W), BF16)],
        compiler_params=_params("parallel"),
        name="ctxproj",
    )(ctx, mod3, g, w_bf, w_bf)


def _attn_kernel(q_ref, k_ref, v_ref, kc_ref, vc_ref, lq1_ref, lk1_ref, lq2_ref, lk2_ref,
                 sg_ref, o_ref, *, lam_init):
    lam = (jnp.exp(jnp.sum(lq1_ref[...] * lk1_ref[...], keepdims=True))
           - jnp.exp(jnp.sum(lq2_ref[...] * lk2_ref[...], keepdims=True)) + lam_init)
    lane = lax.broadcasted_iota(I32, (1, V_DIM), 1)
    first = (lane < HEAD_DIM).astype(BF16)
    second = (lane >= HEAD_DIM).astype(BF16)
    sg = sg_ref[...] * (1.0 - lam_init)

    for h, u in [(h, u) for u in range(TQ_BLOCK // TQ) for h in range(HEADS)]:
        sl = slice(h * V_DIM, (h + 1) * V_DIM)
        sv = slice(h * 2 * V_DIM, (h + 1) * 2 * V_DIM)
        rows = slice(u * TQ, (u + 1) * TQ)
        qh = q_ref[0, rows, sl]
        q2 = jnp.concatenate([qh * first, qh * second], axis=0)
        s_l = lax.dot_general(q2, k_ref[0, :, sl], NT_DIMS, preferred_element_type=F32)
        s_c = lax.dot_general(q2, kc_ref[0, :, sl], NT_DIMS, preferred_element_type=F32)
        m = jnp.maximum(jnp.max(s_l, axis=1, keepdims=True), jnp.max(s_c, axis=1, keepdims=True))
        p_l = jnp.exp2(s_l - m).astype(BF16)
        p_c = jnp.exp2(s_c - m).astype(BF16)
        pv = (jnp.dot(p_l, v_ref[0, :, sv], preferred_element_type=F32)
              + jnp.dot(p_c, vc_ref[0, :, sv], preferred_element_type=F32))
        num = pv[:, :V_DIM]
        den = pv[:, V_DIM:V_DIM + 1]
        o = num[:TQ] * (1.0 / den[:TQ]) - num[TQ:] * (lam / den[TQ:])
        o_ref[0, rows, sl] = _rms(o, sg).astype(BF16)


def _attn(q, k, v, kc, vc, lq1, lk1, lq2, lk2, sg, lam_init):
    b, t, _ = q.shape
    qt = pl.BlockSpec((1, TQ_BLOCK, QK_W), lambda bi, i: (bi, i, 0))
    vec = pl.BlockSpec((1, HEAD_DIM), lambda bi, i: (0, 0))

    def per_batch(rows, width):
        return pl.BlockSpec((1, rows, width), lambda bi, i: (bi, 0, 0))

    return pl.pallas_call(
        functools.partial(_attn_kernel, lam_init=lam_init),
        grid=(b, t // TQ_BLOCK),
        in_specs=[qt, per_batch(t, QK_W), per_batch(t, 2 * V_W),
                  per_batch(CTX_LEN, QK_W), per_batch(CTX_LEN, 2 * V_W),
                  vec, vec, vec, vec, pl.BlockSpec((1, V_DIM), lambda bi, i: (0, 0))],
        out_specs=qt,
        out_shape=jax.ShapeDtypeStruct((b, t, V_W), BF16),
        compiler_params=_params("parallel", "parallel"),
        name="attn",
    )(q, k, v, kc, vc, lq1, lk1, lq2, lk2, sg)


def _out_kernel(a_ref, c_ref, x_ref, mod_ref, gpost_ref, gpre_ref, wo_ref, wr_ref,
                h_ref, xn_ref, aff_ref):
    mix = (jnp.dot(a_ref[0], wo_ref[0:V_W, :], preferred_element_type=F32)
           + jnp.dot(c_ref[0], wo_ref[V_W:V_W + CONV_W, :], preferred_element_type=F32))
    g1 = mod_ref[0, :, 2 * D_MODEL:3 * D_MODEL]
    sh2 = mod_ref[0, :, 3 * D_MODEL:4 * D_MODEL]
    sc2 = mod_ref[0, :, 4 * D_MODEL:5 * D_MODEL]
    h1 = x_ref[0] + g1 * _rms(mix, gpost_ref[...])
    h_ref[0] = h1
    xn2 = _rms(h1, gpre_ref[...]) * (1.0 + sc2) + sh2
    _store_token_major(xn_ref, (0,), xn2)
    def split(a):
        hi = a.astype(BF16)
        return hi, (a - hi.astype(F32)).astype(BF16)

    w_hi, w_lo = split(wr_ref[...])
    x_hi, x_lo = split(xn2)
    both = lax.dot_general(jnp.concatenate([w_hi, w_lo], axis=0), x_hi, NT_DIMS,
                           preferred_element_type=F32)
    logits = (both[:N_EXPERTS] + both[N_EXPERTS:]
              + lax.dot_general(w_hi, x_lo, NT_DIMS, preferred_element_type=F32))
    e = jnp.exp(logits - jnp.max(logits, axis=0, keepdims=True))
    aff_ref[0] = e / jnp.sum(e, axis=0, keepdims=True)


def _outproj(attn, conv, x, mod3, gpost, gpre, wo_bf, wr_t):
    b, t, _ = x.shape
    half = pl.BlockSpec((1, TM, V_W), lambda bi, i: (bi, i, 0))
    tile = pl.BlockSpec((1, TM, D_MODEL), lambda bi, i: (bi, i, 0))
    row = pl.BlockSpec((1, D_MODEL), lambda bi, i: (0, 0))
    act = jax.ShapeDtypeStruct((b, t, D_MODEL), F32)
    return pl.pallas_call(
        _out_kernel,
        grid=(b, t // TM),
        in_specs=[half, half, tile,
                  pl.BlockSpec((1, 1, 6 * D_MODEL), lambda bi, i: (bi, 0, 0)),
                  row, row,
                  pl.BlockSpec((V_W + CONV_W, D_MODEL), lambda bi, i: (0, 0)),
                  pl.BlockSpec((N_EXPERTS, D_MODEL), lambda bi, i: (0, 0))],
        out_specs=[tile, pl.BlockSpec((1, TM * ROW_TILES, LANES), lambda bi, i: (bi, i, 0)),
                   pl.BlockSpec((1, N_EXPERTS, TM), lambda bi, i: (bi, 0, i))],
        out_shape=[act, jax.ShapeDtypeStruct((b, t * ROW_TILES, LANES), F32),
                   jax.ShapeDtypeStruct((b, N_EXPERTS, t), F32)],
        compiler_params=_params("parallel", "parallel"),
        name="outproj",
    )(attn, conv, x, mod3, gpost, gpre, wo_bf, wr_t)


def _prefix_exclusive(mask_f):
    r = lax.broadcasted_iota(I32, (LANES, LANES), 0)
    c = lax.broadcasted_iota(I32, (LANES, LANES), 1)
    upper = (r <= c).astype(BF16)
    off = jnp.zeros((mask_f.shape[0], 1), F32)
    outs = []
    for j in range(mask_f.shape[1] // LANES):
        blk = mask_f[:, j * LANES:(j + 1) * LANES]
        inc = jnp.dot(blk.astype(BF16), upper, preferred_element_type=F32)
        outs.append(inc - blk + off)
        off = off + inc[:, LANES - 1:LANES]
    return jnp.concatenate(outs, axis=1)


def _topk_kernel(aff_ref, idx_ref, gate_ref, *, cap):
    aff = aff_ref[0]
    n_e, t = aff.shape

    def search(i, thr_bits):
        cand = thr_bits | (jnp.int32(1) << (29 - i))
        cnt = jnp.sum((aff >= pltpu.bitcast(cand, F32)).astype(F32), axis=1, keepdims=True)
        return jnp.where(cnt >= cap, cand, thr_bits)

    thr = pltpu.bitcast(lax.fori_loop(0, 30, search, jnp.zeros((n_e, 1), I32)), F32)
    gt = aff > thr
    eq = aff == thr
    need = cap - jnp.sum(gt.astype(F32), axis=1, keepdims=True)
    eq_f = eq.astype(F32)
    take_eq = jnp.where(_prefix_exclusive(eq_f) < need, eq_f, 0.0)
    sel_f = jnp.where(gt, 1.0, take_eq)
    slot = jnp.where(sel_f > 0.0, _prefix_exclusive(sel_f), -1.0)

    tok = lax.broadcasted_iota(I32, (1, t), 1)
    tok_hi = (tok >> TOK_RADIX_BITS).astype(F32)
    tok_lo = (tok & (TOK_RADIX - 1)).astype(F32)
    sub = lax.broadcasted_iota(I32, (8, t), 0)
    slot_b = slot.astype(BF16)
    slot_ids = lax.broadcasted_iota(I32, (cap, 1), 0).astype(F32).astype(BF16)
    one_b = jnp.ones((cap, t), BF16)
    zero_b = jnp.zeros((cap, t), BF16)
    lane = lax.broadcasted_iota(I32, (cap, LANES), 1)
    idx_acc = jnp.zeros((cap, LANES), F32)
    gate_acc = jnp.zeros((cap, LANES), F32)
    for e in range(n_e):
        g = aff[e:e + 1, :]
        g_hi = g.astype(BF16).astype(F32)
        g_mid = (g - g_hi).astype(BF16).astype(F32)
        g_lo = g - g_hi - g_mid
        cols = jnp.zeros((8, t), F32)
        for k, row in enumerate((tok_hi, tok_lo, g_hi, g_mid, g_lo)):
            cols = jnp.where(sub == k, row, cols)
        hit = jnp.where(slot_b[e:e + 1, :] == slot_ids, one_b, zero_b)
        picked = lax.dot_general(hit, cols.astype(BF16), NT_DIMS, preferred_element_type=F32)
        idx_col = picked[:, 0:1] * TOK_RADIX + picked[:, 1:2]
        gate_col = picked[:, 2:3] + picked[:, 3:4] + picked[:, 4:5]
        idx_acc = jnp.where(lane == e, idx_col, idx_acc)
        gate_acc = jnp.where(lane == e, gate_col, gate_acc)
    idx_ref[0] = idx_acc.astype(I32)
    gate_ref[0] = gate_acc


def _topk(aff_t, cap):
    b, n_e, t = aff_t.shape
    out = pl.BlockSpec((1, cap, LANES), lambda bi: (bi, 0, 0))
    return pl.pallas_call(
        functools.partial(_topk_kernel, cap=cap),
        grid=(b,),
        in_specs=[pl.BlockSpec((1, n_e, t), lambda bi: (bi, 0, 0))],
        out_specs=[out, out],
        out_shape=[jax.ShapeDtypeStruct((b, cap, LANES), I32),
                   jax.ShapeDtypeStruct((b, cap, LANES), F32)],
        compiler_params=_params("parallel"),
        name="topk",
    )(aff_t)


def _ffn_kernel(idx_ref, idx_next_ref, xn_hbm, wg_ref, wu_ref, wd_ref, y_ref, xs_ref, sems, *, cap):
    nb = pl.num_programs(1)
    step = pl.program_id(0) * nb + pl.program_id(1)
    last = pl.num_programs(0) * nb - 1
    rows = FFN_G * cap
    slot = step % 2

    def row_copy(ids_ref, batch0, k, r, dst_slot):
        src_row = pl.multiple_of(ids_ref[k, 0, 0, r], ROW_TILES)
        return pltpu.make_async_copy(xn_hbm.at[batch0 + k, pl.ds(src_row, ROW_TILES), :],
                                     xs_ref.at[dst_slot, pl.ds((k * cap + r) * ROW_TILES, ROW_TILES), :],
                                     sems.at[dst_slot])

    def wait_slot(s):
        pltpu.make_async_copy(xn_hbm.at[0, pl.ds(0, 2 * rows * ROW_TILES), :], xs_ref.at[s], sems.at[s]).wait()

    def expert(half):
        xs = jnp.concatenate(
            [xs_ref[slot, pl.ds(half * rows * ROW_TILES + j, rows, stride=ROW_TILES), :]
             for j in range(ROW_TILES)], axis=1)
        hg = jnp.dot(xs, wg_ref[0], preferred_element_type=F32, precision=lax.Precision.DEFAULT)
        hu = jnp.dot(xs, wu_ref[0], preferred_element_type=F32, precision=lax.Precision.DEFAULT)
        y = jnp.dot(_silu(hg) * hu, wd_ref[0], preferred_element_type=F32, precision=lax.Precision.DEFAULT)
        for g in range(FFN_G):
            _store_token_major(y_ref, (half * FFN_G + g, 0), y[g * cap:(g + 1) * cap])

    @pl.when(step == 0)
    def _():
        for k in range(2 * FFN_G):
            def issue(r, carry, k=k):
                row_copy(idx_ref, 0, k, r, 0).start()
                return carry
            lax.fori_loop(0, cap, issue, 0)

    wait_slot(slot)
    expert(0)
    nxt_batch0 = (jnp.minimum(step + 1, last) % nb) * (2 * FFN_G)
    for k in range(2 * FFN_G):
        for r in range(cap):
            row_copy(idx_next_ref, nxt_batch0, k, r, 1 - slot).start(priority=r % 2)
    expert(1)

    @pl.when(step == last)
    def _():
        wait_slot(1 - slot)


def _ffn(rows4, xn2, wg, wu, wd, cap):
    b = xn2.shape[0]
    n_e, d, _ = wg.shape
    blk = 2 * FFN_G
    nb = b // blk
    wspec = pl.BlockSpec((1, d, d), lambda e, bg: (e, 0, 0))

    def next_ids(e, bg):
        nxt = jnp.minimum(e * nb + bg + 1, n_e * nb - 1)
        return (nxt % nb, nxt // nb, 0, 0)

    return pl.pallas_call(
        functools.partial(_ffn_kernel, cap=cap),
        grid=(n_e, nb),
        in_specs=[
            pl.BlockSpec((blk, 1, 1, cap), lambda e, bg: (bg, e, 0, 0), memory_space=pltpu.SMEM),
            pl.BlockSpec((blk, 1, 1, cap), next_ids, memory_space=pltpu.SMEM),
            pl.BlockSpec(memory_space=pl.ANY),
            wspec, wspec, wspec,
        ],
        out_specs=pl.BlockSpec((blk, 1, cap * ROW_TILES, LANES), lambda e, bg: (bg, e, 0, 0)),
        out_shape=jax.ShapeDtypeStruct((b, n_e, cap * ROW_TILES, LANES), F32),
        scratch_shapes=[pltpu.VMEM((2, 2 * FFN_G * cap * ROW_TILES, LANES), F32),
                        pltpu.SemaphoreType.DMA((2,))],
        compiler_params=_params("arbitrary", "arbitrary"),
        name="ffn",
    )(rows4, rows4, xn2, wg, wu, wd)


def _combine_kernel(row_ref, gate_ref, y_ref, h_ref, mod_ref, g_ref, o_ref, ff_scr, *, cap):
    group = pl.program_id(1)

    @pl.when(group == 0)
    def _():
        ff_scr[...] = jnp.zeros_like(ff_scr)

    for e in range(COMBINE_EXPERTS):
        for r0 in range(0, cap, COMBINE_UNROLL):
            dst = [pl.ds(pl.multiple_of(row_ref[0, e, 0, r0 + u], ROW_TILES), ROW_TILES)
                   for u in range(COMBINE_UNROLL)]
            sums = [ff_scr[dst[u], :]
                    + y_ref[0, e, (r0 + u) * ROW_TILES:(r0 + u + 1) * ROW_TILES, :] * gate_ref[0, e, 0, r0 + u]
                    for u in range(COMBINE_UNROLL)]
            for u in range(COMBINE_UNROLL):
                ff_scr[dst[u], :] = sums[u]

    @pl.when(group == pl.num_programs(1) - 1)
    def _():
        g2 = mod_ref[0, :, 5 * D_MODEL:6 * D_MODEL]
        g = g_ref[...]
        for c in range(h_ref.shape[1] // FINAL_ROWS):
            rows = slice(c * FINAL_ROWS, (c + 1) * FINAL_ROWS)
            ff = jnp.concatenate(
                [ff_scr[pl.ds(c * FINAL_ROWS * ROW_TILES + j, FINAL_ROWS, stride=ROW_TILES), :]
                 for j in range(ROW_TILES)], axis=1)
            o_ref[0, rows, :] = h_ref[0, rows, :] + g2 * _rms(ff, g)


def _combine(rows4, gate4, y, h1, mod3, g, cap):
    b, t, d = h1.shape
    n_e = y.shape[1]
    scalars = pl.BlockSpec((1, COMBINE_EXPERTS, 1, cap), lambda bi, e: (bi, e, 0, 0), memory_space=pltpu.SMEM)
    whole = pl.BlockSpec((1, t, d), lambda bi, e: (bi, 0, 0))
    return pl.pallas_call(
        functools.partial(_combine_kernel, cap=cap),
        grid=(b, n_e // COMBINE_EXPERTS),
        in_specs=[scalars, scalars,
                  pl.BlockSpec((1, COMBINE_EXPERTS, cap * ROW_TILES, LANES), lambda bi, e: (bi, e, 0, 0)),
                  whole,
                  pl.BlockSpec((1, 1, 6 * d), lambda bi, e: (bi, 0, 0)),
                  pl.BlockSpec((1, d), lambda bi, e: (0, 0))],
        out_specs=pl.BlockSpec((1, t, d), lambda bi, e: (bi, 0, 0), pipeline_mode=pl.Buffered(1)),
        out_shape=jax.ShapeDtypeStruct((b, t, d), F32),
        scratch_shapes=[pltpu.VMEM((t * ROW_TILES, LANES), F32)],
        compiler_params=pltpu.CompilerParams(dimension_semantics=("parallel", "arbitrary"),
                                             vmem_limit_bytes=COMBINE_VMEM_LIMIT),
        name="combine",
    )(rows4, gate4, y, h1, mod3, g)


def _rope_tables(t):
    rows = t // GRID_W
    row = jnp.repeat(jnp.arange(rows, dtype=F32), GRID_W)
    col = jnp.tile(jnp.arange(GRID_W, dtype=F32), rows)
    inv = ROPE_BASE ** (-jnp.arange(0, ROT_AXIS, 2, dtype=F32) / ROT_AXIS)
    ang_r = row[:, None] * inv[None, :]
    ang_c = col[:, None] * inv[None, :]
    ang = jnp.concatenate([ang_r, ang_r, ang_c, ang_c], axis=-1)
    ang = jnp.concatenate([ang, ang], axis=-1)
    cos, sin = jnp.cos(ang), jnp.sin(ang)
    low = (jnp.arange(LANES) % ROT_AXIS) < ROT_AXIS // 2
    sin_lo = jnp.where(low[None, :], -sin, 0.0)
    sin_hi = jnp.where(low[None, :], 0.0, sin)
    scale = jnp.float32(math.log2(math.e) / math.sqrt(HEAD_DIM))
    return jnp.concatenate([cos * scale, sin_lo * scale, sin_hi * scale, cos, sin_lo, sin_hi], axis=-1)


def kernel(x, c, ctx, c_ctx, w_ada, b_ada, norm_pre_mix, norm_post_mix, norm_pre_ffn, norm_post_ffn,
           w_in, conv_w, lambda_q1, lambda_k1, lambda_q2, lambda_k2, subln_g, w_out, w_router,
           w_gate, w_up, w_down):
    b, t, d = x.shape
    depth = w_ada.shape[0]
    assert depth == 1 and d == D_MODEL and t % TM == 0 and t % TQ_BLOCK == 0 and TQ_BLOCK % TQ == 0 and b % (2 * FFN_G) == 0
    assert b + 1 <= MOD_ROWS
    layer = 0
    lam_init = 0.8 - 0.6 * math.exp(-0.3 * layer)
    cap = CAP_FACTOR * t // N_EXPERTS

    cs = jnp.concatenate([c, c_ctx[None, :], jnp.zeros((MOD_ROWS - b - 1, d), F32)], axis=0)
    mod3 = _ada(cs, w_ada[layer], b_ada[layer][None, :]).reshape(MOD_ROWS, 1, 6 * d)

    w_in_bf = w_in[layer].astype(BF16)
    q, k, v, conv = _inproj(x, mod3, norm_pre_mix[layer][None, :], w_in_bf, conv_w[layer], _rope_tables(t))
    kc, vc = _ctxproj(ctx, mod3, norm_pre_mix[layer][None, :], w_in_bf)
    attn = _attn(q, k, v, kc, vc, lambda_q1[layer][None, :], lambda_k1[layer][None, :],
                 lambda_q2[layer][None, :], lambda_k2[layer][None, :], subln_g[layer][None, :], lam_init)
    h1, xn2, aff_t = _outproj(attn, conv, x, mod3, norm_post_mix[layer][None, :],
                              norm_pre_ffn[layer][None, :], w_out[layer].astype(BF16), w_router[layer].T)

    idx_p, gate_p = _topk(aff_t, cap)
    rows4 = (jnp.swapaxes(idx_p[:, :, :N_EXPERTS], 1, 2) * ROW_TILES).reshape(b, N_EXPERTS, 1, cap)
    gate4 = jnp.swapaxes(gate_p[:, :, :N_EXPERTS], 1, 2).reshape(b, N_EXPERTS, 1, cap)

    y = _ffn(rows4, xn2, w_gate[layer], w_up[layer], w_down[layer], cap)
    return _combine(rows4, gate4, y, h1, mod3, norm_post_ffn[layer][None, :], cap)
```
